```python
import math
import jax, jax.numpy as jnp
from jax import lax
import numpy as np

D_MODEL = 1024
BATCH = 8
SEQ = 4096
DEPTH = 1
DEC_BATCH = 8
DEC_SEQ = 2048
PAST_LEN = 128

D_FF = 2816
D_SSM = D_MODEL // 2
SSM_GROUP = 16
N_SSM_GROUPS = D_SSM // SSM_GROUP
SSM_STATE = 64
N_HEADS = 8
QK_NOPE = 64
QK_ROPE = 32
V_HEAD = 64
D_ATTN = N_HEADS * V_HEAD
Q_RANK = (3 * D_MODEL) // 8
KV_RANK = D_MODEL // 4
D_IN = D_SSM + Q_RANK + KV_RANK + QK_ROPE
D_MIX = D_SSM + D_ATTN
Q_BLOCK = 128
ROPE_THETA = 10000.0
EPS = 1e-6
DT_MIN = 1e-3
DT_MAX = 1e-1

kernel_name = 'hybrid_s5_mla_macaron_encoder'

F32 = jnp.float32


def rms_norm(x, g):
    xf = x.astype(F32)
    y = xf * lax.rsqrt(jnp.mean(xf * xf, axis=-1, keepdims=True) + EPS)
    return (y * g.astype(F32)).astype(x.dtype)


def swiglu(x, w_gate, w_up, w_down):
    return (jax.nn.silu(x @ w_gate) * (x @ w_up)) @ w_down


def rope_tables(length):
    inv = 1.0 / (ROPE_THETA ** (jnp.arange(0, QK_ROPE, 2, dtype=F32) / QK_ROPE))
    ang = jnp.arange(length, dtype=F32)[:, None] * inv[None, :]
    return jnp.cos(ang), jnp.sin(ang)


def apply_rope(x, cos, sin):
    x1, x2 = jnp.split(x.astype(F32), 2, axis=-1)
    return jnp.concatenate([x1 * cos - x2 * sin, x1 * sin + x2 * cos], axis=-1).astype(x.dtype)


def _linear_recurrence(e1, e2):
    a1, b1 = e1
    a2, b2 = e2
    return a1 * a2, a2 * b1 + b2


def ssm_direction(u_c, lam_re, lam_im, log_dt, b_re, b_im, c_re, c_im, reverse):
    lam = lax.complex(lam_re.astype(F32), lam_im.astype(F32))
    dt = jnp.exp(log_dt.astype(F32))[:, None]
    lam_bar = jnp.exp(lam * dt)
    b_mat = lax.complex(b_re.astype(F32), b_im.astype(F32))
    b_bar = ((lam_bar - 1.0) / lam)[..., None] * b_mat
    bu = jnp.einsum('blgc,gpc->blgp', u_c, b_bar)
    a = jnp.broadcast_to(lam_bar, bu.shape)
    _, s = lax.associative_scan(_linear_recurrence, (a, bu), axis=1, reverse=reverse)
    c_mat = lax.complex(c_re.astype(F32), c_im.astype(F32))
    return jnp.real(jnp.einsum('blgp,gcp->blgc', s, c_mat))


def s5_mixer(u, fwd, bwd, d_skip, w_glu, b_glu):
    bsz, length, _ = u.shape
    ug = u.reshape(bsz, length, N_SSM_GROUPS, SSM_GROUP).astype(F32)
    u_c = ug.astype(jnp.complex64)
    y = (ssm_direction(u_c, *fwd, reverse=False)
         + ssm_direction(u_c, *bwd, reverse=True)
         + d_skip.astype(F32).reshape(N_SSM_GROUPS, SSM_GROUP) * ug)
    y = jax.nn.gelu(y.reshape(bsz, length, D_SSM)).astype(u.dtype)
    return y * jax.nn.sigmoid(y @ w_glu + b_glu)


def mla_mixer(q_c, kv_c, k_rope, g_q, w_uq, g_kv, w_ukv):
    bsz, length, _ = q_c.shape
    q = (rms_norm(q_c, g_q) @ w_uq).reshape(bsz, length, N_HEADS, QK_NOPE + QK_ROPE)
    kv = (rms_norm(kv_c, g_kv) @ w_ukv).reshape(bsz, length, N_HEADS, QK_NOPE + V_HEAD)
    cos, sin = rope_tables(length)
    scale = (QK_NOPE + QK_ROPE) ** -0.5
    q_nope = q[..., :QK_NOPE] * scale
    q_pe = apply_rope(q[..., QK_NOPE:], cos[:, None, :], sin[:, None, :]) * scale
    k_pe = apply_rope(k_rope, cos, sin)
    k_nope, v = kv[..., :QK_NOPE], kv[..., QK_NOPE:]
    n_blk = length // Q_BLOCK
    qn_b = q_nope.reshape(bsz, n_blk, Q_BLOCK, N_HEADS, QK_NOPE).transpose(1, 0, 2, 3, 4)
    qp_b = q_pe.reshape(bsz, n_blk, Q_BLOCK, N_HEADS, QK_ROPE).transpose(1, 0, 2, 3, 4)

    def attend(blk):
        qn, qp = blk
        s = (jnp.einsum('bqhd,bkhd->bhqk', qn, k_nope, preferred_element_type=F32)
             + jnp.einsum('bqhr,bkr->bhqk', qp, k_pe, preferred_element_type=F32))
        p = jax.nn.softmax(s, axis=-1).astype(v.dtype)
        return jnp.einsum('bhqk,bkhd->bqhd', p, v)

    o = lax.map(attend, (qn_b, qp_b))
    return o.transpose(1, 0, 2, 3, 4).reshape(bsz, length, D_ATTN)


def setup_inputs(seed: int = 0) -> dict:
    key = jax.random.key(seed)
    ks = iter(jax.random.split(key, 64))

    def nrm(shape, scale):
        return scale * jax.random.normal(next(ks), shape, dtype=F32)

    def gain(n):
        return 1.0 + nrm((DEPTH, n), 0.02)

    G, P, C = N_SSM_GROUPS, SSM_STATE, SSM_GROUP

    def ssm_dir():
        lam_re = -0.5 + nrm((DEPTH, G, P), 0.01)
        lam_im = math.pi * jnp.arange(P, dtype=F32)[None, None, :] + nrm((DEPTH, G, P), 0.01)
        log_dt = jax.random.uniform(next(ks), (DEPTH, G), F32, math.log(DT_MIN), math.log(DT_MAX))
        b_re = nrm((DEPTH, G, P, C), (2.0 * C) ** -0.5)
        b_im = nrm((DEPTH, G, P, C), (2.0 * C) ** -0.5)
        c_re = nrm((DEPTH, G, C, P), (2.0 * P) ** -0.5)
        c_im = nrm((DEPTH, G, C, P), (2.0 * P) ** -0.5)
        return lam_re, lam_im, log_dt, b_re, b_im, c_re, c_im

    x_prompt = nrm((BATCH, SEQ, D_MODEL), 1.0)
    x_sample = nrm((DEC_BATCH, DEC_SEQ, D_MODEL), 1.0)
    g_ffn1_pre = gain(D_MODEL)
    w_ffn1_gate = nrm((DEPTH, D_MODEL, D_FF), D_MODEL ** -0.5)
    w_ffn1_up = nrm((DEPTH, D_MODEL, D_FF), D_MODEL ** -0.5)
    w_ffn1_down = nrm((DEPTH, D_FF, D_MODEL), D_FF ** -0.5)
    g_ffn1_post = gain(D_MODEL)
    g_mix_pre = gain(D_MODEL)
    w_in = nrm((DEPTH, D_MODEL, D_IN), D_MODEL ** -0.5)
    lam_re_fwd, lam_im_fwd, log_dt_fwd, b_re_fwd, b_im_fwd, c_re_fwd, c_im_fwd = ssm_dir()
    lam_re_bwd, lam_im_bwd, log_dt_bwd, b_re_bwd, b_im_bwd, c_re_bwd, c_im_bwd = ssm_dir()
    d_skip = nrm((DEPTH, D_SSM), 1.0)
    w_glu = nrm((DEPTH, D_SSM, D_SSM), D_SSM ** -0.5)
    b_glu = nrm((DEPTH, D_SSM), 0.02)
    g_ssm_out = gain(D_SSM)
    g_q = gain(Q_RANK)
    w_uq = nrm((DEPTH, Q_RANK, N_HEADS * (QK_NOPE + QK_ROPE)), Q_RANK ** -0.5)
    g_kv = gain(KV_RANK)
    w_ukv = nrm((DEPTH, KV_RANK, N_HEADS * (QK_NOPE + V_HEAD)), KV_RANK ** -0.5)
    g_att_out = gain(D_ATTN)
    w_out = nrm((DEPTH, D_MIX, D_MODEL), D_MIX ** -0.5)
    g_mix_post = gain(D_MODEL)
    g_ffn2_pre = gain(D_MODEL)
    w_ffn2_gate = nrm((DEPTH, D_MODEL, D_FF), D_MODEL ** -0.5)
    w_ffn2_up = nrm((DEPTH, D_MODEL, D_FF), D_MODEL ** -0.5)
    w_ffn2_down = nrm((DEPTH, D_FF, D_MODEL), D_FF ** -0.5)
    g_ffn2_post = gain(D_MODEL)
    return {
        'x_prompt': x_prompt, 'x_sample': x_sample,
        'g_ffn1_pre': g_ffn1_pre, 'w_ffn1_gate': w_ffn1_gate, 'w_ffn1_up': w_ffn1_up,
        'w_ffn1_down': w_ffn1_down, 'g_ffn1_post': g_ffn1_post,
        'g_mix_pre': g_mix_pre, 'w_in': w_in,
        'lam_re_fwd': lam_re_fwd, 'lam_im_fwd': lam_im_fwd, 'log_dt_fwd': log_dt_fwd,
        'b_re_fwd': b_re_fwd, 'b_im_fwd': b_im_fwd, 'c_re_fwd': c_re_fwd, 'c_im_fwd': c_im_fwd,
        'lam_re_bwd': lam_re_bwd, 'lam_im_bwd': lam_im_bwd, 'log_dt_bwd': log_dt_bwd,
        'b_re_bwd': b_re_bwd, 'b_im_bwd': b_im_bwd, 'c_re_bwd': c_re_bwd, 'c_im_bwd': c_im_bwd,
        'd_skip': d_skip, 'w_glu': w_glu, 'b_glu': b_glu, 'g_ssm_out': g_ssm_out,
        'g_q': g_q, 'w_uq': w_uq, 'g_kv': g_kv, 'w_ukv': w_ukv, 'g_att_out': g_att_out,
        'w_out': w_out, 'g_mix_post': g_mix_post,
        'g_ffn2_pre': g_ffn2_pre, 'w_ffn2_gate': w_ffn2_gate, 'w_ffn2_up': w_ffn2_up,
        'w_ffn2_down': w_ffn2_down, 'g_ffn2_post': g_ffn2_post,
    }


def reference(x_prompt, x_sample,
              g_ffn1_pre, w_ffn1_gate, w_ffn1_up, w_ffn1_down, g_ffn1_post,
              g_mix_pre, w_in,
              lam_re_fwd, lam_im_fwd, log_dt_fwd, b_re_fwd, b_im_fwd, c_re_fwd, c_im_fwd,
              lam_re_bwd, lam_im_bwd, log_dt_bwd, b_re_bwd, b_im_bwd, c_re_bwd, c_im_bwd,
              d_skip, w_glu, b_glu, g_ssm_out,
              g_q, w_uq, g_kv, w_ukv, g_att_out,
              w_out, g_mix_post,
              g_ffn2_pre, w_ffn2_gate, w_ffn2_up, w_ffn2_down, g_ffn2_post):

    def trunk(x):
        for l in range(DEPTH):
            h = rms_norm(x, g_ffn1_pre[l])
            x = x + 0.5 * rms_norm(swiglu(h, w_ffn1_gate[l], w_ffn1_up[l], w_ffn1_down[l]), g_ffn1_post[l])
            h = rms_norm(x, g_mix_pre[l])
            z = h @ w_in[l]
            u, q_c, kv_c, k_rope = jnp.split(
                z, [D_SSM, D_SSM + Q_RANK, D_SSM + Q_RANK + KV_RANK], axis=-1)
            fwd = (lam_re_fwd[l], lam_im_fwd[l], log_dt_fwd[l], b_re_fwd[l], b_im_fwd[l],
                   c_re_fwd[l], c_im_fwd[l])
            bwd = (lam_re_bwd[l], lam_im_bwd[l], log_dt_bwd[l], b_re_bwd[l], b_im_bwd[l],
                   c_re_bwd[l], c_im_bwd[l])
            y_ssm = rms_norm(s5_mixer(u, fwd, bwd, d_skip[l], w_glu[l], b_glu[l]), g_ssm_out[l])
            y_att = rms_norm(mla_mixer(q_c, kv_c, k_rope, g_q[l], w_uq[l], g_kv[l], w_ukv[l]),
                             g_att_out[l])
            m = jnp.concatenate([y_ssm, y_att], axis=-1) @ w_out[l]
            x = x + rms_norm(m, g_mix_post[l])
            h = rms_norm(x, g_ffn2_pre[l])
            x = x + 0.5 * rms_norm(swiglu(h, w_ffn2_gate[l], w_ffn2_up[l], w_ffn2_down[l]), g_ffn2_post[l])
        return x

    y_prompt = trunk(x_prompt)
    y_sample = trunk(x_sample)
    return (y_prompt, y_sample)
```

```python
import functools
import math

import jax
import jax.numpy as jnp
from jax import lax
from jax.experimental import pallas as pl
from jax.experimental.pallas import tpu as pltpu

F32 = jnp.float32
BF16 = jnp.bfloat16

D_MODEL = 1024
D_FF = 2816
D_SSM = 512
SSM_GROUP = 16
N_SSM_GROUPS = 32
SSM_STATE = 64
N_HEADS = 8
QK_NOPE = 64
QK_ROPE = 32
V_HEAD = 64
D_ATTN = N_HEADS * V_HEAD
Q_RANK = 384
KV_RANK = 256
ROPE_THETA = 10000.0
EPS = 1e-6
LOG2E = math.log2(math.e)

LANES = 128
SUBLANES = 8
VMEM_LIMIT_BYTES = 56 * 1024 * 1024

HEAD_PAD = LANES
D_HEADS_PAD = N_HEADS * HEAD_PAD
D_IN_PAD = D_SSM + Q_RANK + KV_RANK + LANES
FF_CHUNK = D_FF // 2

TM = 512
TQ = 256
TC = 64
GROUPS_PER_TILE = LANES // SSM_GROUP
N_QUARTERS = D_SSM // LANES
STATE_TILE = GROUPS_PER_TILE * SSM_STATE


def _const_spec(shape):
  zeros = (0,) * len(shape)
  return pl.BlockSpec(shape, lambda *_: zeros, pipeline_mode=pl.Buffered(1))


def _rms(x, g):
  ms = jnp.mean(x * x, axis=-1, keepdims=True)
  return x * lax.rsqrt(ms + EPS) * g


def _swiglu(h, wgu_ref, wd_ref):
  acc = None
  for c in range(D_FF // FF_CHUNK):
    lo = c * FF_CHUNK
    gate = jnp.dot(h, wgu_ref[:, lo:lo + FF_CHUNK], preferred_element_type=F32)
    up = jnp.dot(h, wgu_ref[:, D_FF + lo:D_FF + lo + FF_CHUNK], preferred_element_type=F32)
    act = (gate * jax.nn.sigmoid(gate) * up).astype(BF16)
    part = jnp.dot(act, wd_ref[lo:lo + FF_CHUNK, :], preferred_element_type=F32)
    acc = part if acc is None else acc + part
  return acc


def _pre_kernel(x_ref, cq_ref, sq_ref, ck_ref, sk_ref,
                g1pre_ref, wgu_ref, wd_ref, g1post_ref, gmix_ref, win_ref,
                gq_ref, wuq_ref, gkv_ref, wukv_ref,
                x1_ref, u_ref, q_ref, k_ref, v_ref):
  x = x_ref[...]
  h = _rms(x, g1pre_ref[...]).astype(BF16)
  f = _swiglu(h, wgu_ref, wd_ref)
  x1 = x + 0.5 * _rms(f, g1post_ref[...])
  x1_ref[...] = x1

  h2 = _rms(x1, gmix_ref[...]).astype(BF16)
  z = jnp.dot(h2, win_ref[...], preferred_element_type=F32)
  u_ref[...] = z[:, :D_SSM]
  q_c = z[:, D_SSM:D_SSM + Q_RANK]
  kv_c = z[:, D_SSM + Q_RANK:D_SSM + Q_RANK + KV_RANK]
  kr = z[:, D_SSM + Q_RANK + KV_RANK:]

  lane = lax.broadcasted_iota(jnp.int32, kr.shape, 1)
  half = QK_ROPE // 2

  kr_sw = jnp.where(lane < half, pltpu.roll(kr, LANES - half, 1), pltpu.roll(kr, half, 1))
  k_pe = pltpu.roll(kr * ck_ref[...] + kr_sw * sk_ref[...], QK_NOPE, 1)

  qn = _rms(q_c, gq_ref[...]).astype(BF16)
  q_raw = jnp.dot(qn, wuq_ref[...], preferred_element_type=F32)
  kvn = _rms(kv_c, gkv_ref[...]).astype(BF16)
  kv_raw = jnp.dot(kvn, wukv_ref[...], preferred_element_type=F32)

  cq = cq_ref[...]
  sq = sq_ref[...]
  ones_col = (lane == V_HEAD).astype(F32)
  for hd in range(N_HEADS):
    lo = hd * HEAD_PAD
    t = q_raw[:, lo:lo + HEAD_PAD]
    t_sw = jnp.where(lane < QK_NOPE + half, pltpu.roll(t, LANES - half, 1), pltpu.roll(t, half, 1))
    q_ref[:, lo:lo + HEAD_PAD] = (t * cq + t_sw * sq).astype(BF16)
    k_ref[:, lo:lo + HEAD_PAD] = (kv_raw[:, lo:lo + HEAD_PAD] + k_pe).astype(BF16)
    v_ref[:, lo:lo + HEAD_PAD] = (
        kv_raw[:, D_HEADS_PAD + lo:D_HEADS_PAD + lo + HEAD_PAD] + ones_col).astype(BF16)


def _pre_call(x2d, seq_len, tabs, w):
  n = x2d.shape[0]
  nlt = seq_len // TM
  row = lambda i: (i, 0)
  pos = lambda i: (i % nlt, 0)
  tab_spec = pl.BlockSpec((TM, LANES), pos)
  in_specs = [
      pl.BlockSpec((TM, D_MODEL), row), tab_spec, tab_spec, tab_spec, tab_spec,
      _const_spec((1, D_MODEL)), _const_spec((D_MODEL, 2 * D_FF)), _const_spec((D_FF, D_MODEL)),
      _const_spec((1, D_MODEL)), _const_spec((1, D_MODEL)), _const_spec((D_MODEL, D_IN_PAD)),
      _const_spec((1, Q_RANK)), _const_spec((Q_RANK, D_HEADS_PAD)),
      _const_spec((1, KV_RANK)), _const_spec((KV_RANK, 2 * D_HEADS_PAD)),
  ]
  out_shape = [
      jax.ShapeDtypeStruct((n, D_MODEL), F32), jax.ShapeDtypeStruct((n, D_SSM), F32),
      jax.ShapeDtypeStruct((n, D_HEADS_PAD), BF16), jax.ShapeDtypeStruct((n, D_HEADS_PAD), BF16),
      jax.ShapeDtypeStruct((n, D_HEADS_PAD), BF16),
  ]
  out_specs = [
      pl.BlockSpec((TM, D_MODEL), row), pl.BlockSpec((TM, D_SSM), row),
      pl.BlockSpec((TM, D_HEADS_PAD), row), pl.BlockSpec((TM, D_HEADS_PAD), row),
      pl.BlockSpec((TM, D_HEADS_PAD), row),
  ]
  return pl.pallas_call(
      _pre_kernel, grid=(n // TM,), in_specs=in_specs, out_specs=out_specs, out_shape=out_shape,
      compiler_params=pltpu.CompilerParams(
          dimension_semantics=("arbitrary",), vmem_limit_bytes=VMEM_LIMIT_BYTES),
      name="pre",
  )(x2d, tabs["cq"], tabs["sq"], tabs["ck"], tabs["sk"],
    w["g_ffn1_pre"], w["wgu1"], w["wd1"], w["g_ffn1_post"], w["g_mix_pre"], w["w_in"],
    w["g_q"], w["w_uq"], w["g_kv"], w["w_ukv"])


def _attn_kernel(q_ref, k_ref, v_ref, g_ref, o_ref):
  tq = q_ref.shape[0]
  lane = lax.broadcasted_iota(jnp.int32, (tq, HEAD_PAD), 1)
  heads = []
  ssq = jnp.zeros((tq, 1), F32)
  for hd in range(N_HEADS):
    lo = hd * HEAD_PAD
    s = lax.dot_general(q_ref[:, lo:lo + HEAD_PAD], k_ref[:, lo:lo + HEAD_PAD],
                        (((1,), (1,)), ((), ())), preferred_element_type=F32)
    m = jnp.max(s, axis=-1, keepdims=True)
    p = jnp.exp2(s - m).astype(BF16)
    oa = jnp.dot(p, v_ref[:, lo:lo + HEAD_PAD], preferred_element_type=F32)
    o = jnp.where(lane < V_HEAD, oa / oa[:, V_HEAD:V_HEAD + 1], 0.0)
    ssq = ssq + jnp.sum(o * o, axis=-1, keepdims=True)
    heads.append(o)
  inv = lax.rsqrt(ssq * (1.0 / D_ATTN) + EPS)
  for j in range(N_HEADS // 2):
    pair = heads[2 * j] + pltpu.roll(heads[2 * j + 1], V_HEAD, 1)
    o_ref[:, j * LANES:(j + 1) * LANES] = (
        pair * inv * g_ref[:, j * LANES:(j + 1) * LANES]).astype(BF16)


def _attn_call(q, k, v, g_att, batch, seq_len):
  n = q.shape[0]
  nqt = seq_len // TQ
  kv_spec = pl.BlockSpec((seq_len, D_HEADS_PAD), lambda b, i: (b, 0), pipeline_mode=pl.Buffered(1))
  return pl.pallas_call(
      _attn_kernel, grid=(batch, nqt),
      in_specs=[pl.BlockSpec((TQ, D_HEADS_PAD), lambda b, i: (b * nqt + i, 0)), kv_spec, kv_spec,
                _const_spec((1, D_ATTN))],
      out_specs=pl.BlockSpec((TQ, D_ATTN), lambda b, i: (b * nqt + i, 0)),
      out_shape=jax.ShapeDtypeStruct((n, D_ATTN), BF16),
      compiler_params=pltpu.CompilerParams(
          dimension_semantics=("arbitrary", "arbitrary"), vmem_limit_bytes=VMEM_LIMIT_BYTES),
      name="attn",
  )(q, k, v, g_att)


def _disc_kernel(lre_ref, lim_ref, ldt_ref, bre_ref, bim_ref, are_ref, aim_ref, bbre_ref, bbim_ref):
  lre = lre_ref[...]
  lim = lim_ref[...]
  dt = jnp.exp(ldt_ref[...])
  mag = jnp.exp(lre * dt)
  ar = mag * jnp.cos(lim * dt)
  ai = mag * jnp.sin(lim * dt)
  are_ref[...] = ar
  aim_ref[...] = ai
  den = lre * lre + lim * lim
  cr = ((ar - 1.0) * lre + ai * lim) / den
  ci = (ai * lre - (ar - 1.0) * lim) / den
  bre = bre_ref[...]
  bim = bim_ref[...]
  bbre_ref[...] = cr * bre - ci * bim
  bbim_ref[...] = cr * bim + ci * bre


def _discretise(lam_re, lam_im, log_dt, b_re_t, b_im_t):
  n, c, p = b_re_t.shape
  return pl.pallas_call(
      _disc_kernel,
      out_shape=[jax.ShapeDtypeStruct((n, 1, p), F32), jax.ShapeDtypeStruct((n, 1, p), F32),
                 jax.ShapeDtypeStruct((n, c, p), F32), jax.ShapeDtypeStruct((n, c, p), F32)],
      name="s5_discretise",
  )(lam_re, lam_im, log_dt, b_re_t, b_im_t)


def _ssm_kernel(*refs, tc, reverse):
  nb = SUBLANES
  u_ref, bmat_ref, cre_ref, cim_ref, lam_ref, y_ref, ut_ref, st_ref, yt_ref, carry_ref = refs

  @pl.when(pl.program_id(0) == 0)
  def _():
    carry_ref[...] = jnp.zeros_like(carry_ref)

  for b in range(nb):
    for q in range(N_QUARTERS):
      ut_ref[q, pl.ds(b, tc, stride=nb), :] = u_ref[b, :, q * LANES:(q + 1) * LANES]
  qw = 2 * STATE_TILE
  for q in range(N_QUARTERS):
    st_ref[:, q * qw:(q + 1) * qw] = jnp.dot(
        ut_ref[q].astype(BF16), bmat_ref[q], preferred_element_type=F32)

  for q in range(N_QUARTERS):
    c_re = q * qw
    c_im = c_re + STATE_TILE
    a_re = lam_ref[0, :, q * STATE_TILE:(q + 1) * STATE_TILE]
    a_im = lam_ref[1, :, q * STATE_TILE:(q + 1) * STATE_TILE]

    def body(k, carry, c_re=c_re, c_im=c_im, a_re=a_re, a_im=a_im):
      s_re, s_im = carry
      t = (tc - 1 - k) if reverse else k
      r = pl.multiple_of(t * nb, nb)
      n_re = a_re * s_re - a_im * s_im + st_ref[pl.ds(r, nb), c_re:c_re + STATE_TILE]
      n_im = a_re * s_im + a_im * s_re + st_ref[pl.ds(r, nb), c_im:c_im + STATE_TILE]
      st_ref[pl.ds(r, nb), c_re:c_re + STATE_TILE] = n_re
      st_ref[pl.ds(r, nb), c_im:c_im + STATE_TILE] = n_im
      return n_re, n_im

    s0 = (carry_ref[:, c_re:c_re + STATE_TILE], carry_ref[:, c_im:c_im + STATE_TILE])
    s_re, s_im = lax.fori_loop(0, tc, body, s0, unroll=4)
    carry_ref[:, c_re:c_re + STATE_TILE] = s_re
    carry_ref[:, c_im:c_im + STATE_TILE] = s_im

  for q in range(N_QUARTERS):
    s_re = st_ref[:, q * qw:q * qw + STATE_TILE].astype(BF16)
    s_im = st_ref[:, q * qw + STATE_TILE:(q + 1) * qw].astype(BF16)
    yt_ref[q] = (jnp.dot(s_re, cre_ref[q], preferred_element_type=F32)
                 - jnp.dot(s_im, cim_ref[q], preferred_element_type=F32))
  for b in range(nb):
    for q in range(N_QUARTERS):
      y_ref[b, :, q * LANES:(q + 1) * LANES] = yt_ref[q, pl.ds(b, tc, stride=nb), :]


def _ssm_call(u, bmat, cre, cim, lam, reverse):
  nb, seq_len, _ = u.shape
  assert nb == SUBLANES
  nblk = seq_len // TC
  blk = (lambda i: (0, nblk - 1 - i, 0)) if reverse else (lambda i: (0, i, 0))
  seq_spec = pl.BlockSpec((nb, TC, D_SSM), blk)
  in_specs = [seq_spec] + [
      _const_spec((N_QUARTERS, LANES, 2 * STATE_TILE)),
      _const_spec((N_QUARTERS, STATE_TILE, LANES)),
      _const_spec((N_QUARTERS, STATE_TILE, LANES)),
      _const_spec((2, nb, N_QUARTERS * STATE_TILE)),
  ]
  rows = TC * nb
  return pl.pallas_call(
      functools.partial(_ssm_kernel, tc=TC, reverse=reverse), grid=(nblk,),
      in_specs=in_specs, out_specs=seq_spec,
      out_shape=jax.ShapeDtypeStruct(u.shape, F32),
      scratch_shapes=[pltpu.VMEM((N_QUARTERS, rows, LANES), F32),
                      pltpu.VMEM((rows, 2 * N_QUARTERS * STATE_TILE), F32),
                      pltpu.VMEM((N_QUARTERS, rows, LANES), F32),
                      pltpu.VMEM((nb, 2 * N_QUARTERS * STATE_TILE), F32)],
      compiler_params=pltpu.CompilerParams(
          dimension_semantics=("arbitrary",), vmem_limit_bytes=VMEM_LIMIT_BYTES),
      name="ssm_bwd" if reverse else "ssm_fwd",
  )(u, bmat, cre, cim, lam)


def _post_kernel(x1_ref, u_ref, yf_ref, yb_ref, ya_ref,
                 dskip_ref, wglu_ref, bglu_ref, gssm_ref, wos_ref, woa_ref, gpost_ref,
                 g2pre_ref, wgu_ref, wd_ref, g2post_ref, o_ref):
  y = yf_ref[...] + yb_ref[...] + dskip_ref[...] * u_ref[...]
  y = y * (0.5 * (1.0 + jnp.tanh(math.sqrt(2.0 / math.pi) * (y + 0.044715 * (y * y * y)))))
  gate = jnp.dot(y.astype(BF16), wglu_ref[...], preferred_element_type=F32) + bglu_ref[...]
  y = y * jax.nn.sigmoid(gate)
  y_ssm = _rms(y, gssm_ref[...]).astype(BF16)
  m = (jnp.dot(y_ssm, wos_ref[...], preferred_element_type=F32)
       + jnp.dot(ya_ref[...], woa_ref[...], preferred_element_type=F32))
  x2 = x1_ref[...] + _rms(m, gpost_ref[...])
  h = _rms(x2, g2pre_ref[...]).astype(BF16)
  f = _swiglu(h, wgu_ref, wd_ref)
  o_ref[...] = x2 + 0.5 * _rms(f, g2post_ref[...])


def _post_call(x1, u, yf, yb, ya, w):
  n = x1.shape[0]
  row = lambda i: (i, 0)
  in_specs = [
      pl.BlockSpec((TM, D_MODEL), row), pl.BlockSpec((TM, D_SSM), row),
      pl.BlockSpec((TM, D_SSM), row), pl.BlockSpec((TM, D_SSM), row), pl.BlockSpec((TM, D_ATTN), row),
      _const_spec((1, D_SSM)), _const_spec((D_SSM, D_SSM)), _const_spec((1, D_SSM)),
      _const_spec((1, D_SSM)), _const_spec((D_SSM, D_MODEL)), _const_spec((D_ATTN, D_MODEL)),
      _const_spec((1, D_MODEL)), _const_spec((1, D_MODEL)),
      _const_spec((D_MODEL, 2 * D_FF)), _const_spec((D_FF, D_MODEL)), _const_spec((1, D_MODEL)),
  ]
  return pl.pallas_call(
      _post_kernel, grid=(n // TM,), in_specs=in_specs,
      out_specs=pl.BlockSpec((TM, D_MODEL), row),
      out_shape=jax.ShapeDtypeStruct((n, D_MODEL), F32),
      compiler_params=pltpu.CompilerParams(
          dimension_semantics=("arbitrary",), vmem_limit_bytes=VMEM_LIMIT_BYTES),
      name="post",
  )(x1, u, yf, yb, ya,
    w["d_skip"], w["w_glu"], w["b_glu"], w["g_ssm_out"], w["w_out_ssm"], w["w_out_att"],
    w["g_mix_post"], w["g_ffn2_pre"], w["wgu2"], w["wd2"], w["g_ffn2_post"])


def _rope_tables(length):
  inv = 1.0 / (ROPE_THETA ** (jnp.arange(0, QK_ROPE, 2, dtype=F32) / QK_ROPE))
  ang = jnp.arange(length, dtype=F32)[:, None] * inv[None, :]
  cos, sin = jnp.cos(ang), jnp.sin(ang)
  half = QK_ROPE // 2
  zk = jnp.zeros((length, LANES - QK_ROPE), F32)
  ck = jnp.concatenate([cos, cos, zk], axis=1)
  sk = jnp.concatenate([-sin, sin, zk], axis=1)
  qs = (QK_NOPE + QK_ROPE) ** -0.5 * LOG2E
  zq = jnp.zeros((length, LANES - QK_NOPE - QK_ROPE), F32)
  cq = qs * jnp.concatenate([jnp.ones((length, QK_NOPE), F32), cos, cos, zq], axis=1)
  sq = qs * jnp.concatenate([jnp.zeros((length, QK_NOPE), F32), -sin, sin, zq], axis=1)
  del half
  return {"cq": cq, "sq": sq, "ck": ck, "sk": sk}


def _head_tiles(wmat, width):
  k = wmat.shape[0]
  wh = wmat.reshape(k, N_HEADS, width)
  return jnp.pad(wh, ((0, 0), (0, 0), (0, HEAD_PAD - width))).reshape(k, D_HEADS_PAD)


def _block_diag_tiles(m):
  eye = jnp.eye(GROUPS_PER_TILE, dtype=m.dtype)
  out = m[:, :, :, None, :] * eye[None, :, None, :, None]
  return out.reshape(N_QUARTERS, GROUPS_PER_TILE * m.shape[2], GROUPS_PER_TILE * m.shape[3])


def _ssm_operands(lam_re, lam_im, log_dt, b_re, b_im, c_re, c_im):
  g, p, c = N_SSM_GROUPS, SSM_STATE, SSM_GROUP
  a_re, a_im, bb_re, bb_im = _discretise(
      lam_re.reshape(g, 1, p), lam_im.reshape(g, 1, p),
      jnp.broadcast_to(log_dt.reshape(g, 1, 1), (g, 1, p)),
      jnp.swapaxes(b_re, 1, 2), jnp.swapaxes(b_im, 1, 2))
  q4 = lambda m: m.reshape(N_QUARTERS, GROUPS_PER_TILE, *m.shape[1:])
  bmat = jnp.concatenate([_block_diag_tiles(q4(bb_re)), _block_diag_tiles(q4(bb_im))], axis=2)
  cre = _block_diag_tiles(q4(jnp.swapaxes(c_re, 1, 2)))
  cim = _block_diag_tiles(q4(jnp.swapaxes(c_im, 1, 2)))
  lam = jnp.stack([a_re.reshape(1, g * p), a_im.reshape(1, g * p)])
  lam = jnp.broadcast_to(lam, (2, SUBLANES, g * p))
  return bmat.astype(BF16), cre.astype(BF16), cim.astype(BF16), lam


def kernel(x_prompt, x_sample, g_ffn1_pre, w_ffn1_gate, w_ffn1_up, w_ffn1_down, g_ffn1_post, g_mix_pre, w_in, lam_re_fwd, lam_im_fwd, log_dt_fwd, b_re_fwd, b_im_fwd, c_re_fwd, c_im_fwd, lam_re_bwd, lam_im_bwd, log_dt_bwd, b_re_bwd, b_im_bwd, c_re_bwd, c_im_bwd, d_skip, w_glu, b_glu, g_ssm_out, g_q, w_uq, g_kv, w_ukv, g_att_out, w_out, g_mix_post, g_ffn2_pre, w_ffn2_gate, w_ffn2_up, w_ffn2_down, g_ffn2_post):
  depth = w_in.shape[0]
  tables = {}

  def trunk(x):
    batch, seq_len, _ = x.shape
    assert batch == SUBLANES and seq_len % TM == 0 and seq_len % TQ == 0 and seq_len % TC == 0
    if seq_len not in tables:
      tables[seq_len] = _rope_tables(seq_len)
    h = x.reshape(batch * seq_len, D_MODEL)
    for l in range(depth):
      row = lambda v: v[l].reshape(1, -1)
      win = w_in[l]
      win_pad = jnp.concatenate(
          [win, jnp.zeros((D_MODEL, D_IN_PAD - win.shape[1]), F32)], axis=1).astype(BF16)
      wukv = w_ukv[l].reshape(KV_RANK, N_HEADS, QK_NOPE + V_HEAD)
      w = {
          "g_ffn1_pre": row(g_ffn1_pre),
          "wgu1": jnp.concatenate([w_ffn1_gate[l], w_ffn1_up[l]], axis=1).astype(BF16),
          "wd1": w_ffn1_down[l].astype(BF16), "g_ffn1_post": row(g_ffn1_post),
          "g_mix_pre": row(g_mix_pre), "w_in": win_pad,
          "g_q": row(g_q), "w_uq": _head_tiles(w_uq[l], QK_NOPE + QK_ROPE).astype(BF16),
          "g_kv": row(g_kv),
          "w_ukv": jnp.concatenate(
              [_head_tiles(wukv[:, :, :QK_NOPE].reshape(KV_RANK, -1), QK_NOPE),
               _head_tiles(wukv[:, :, QK_NOPE:].reshape(KV_RANK, -1), V_HEAD)], axis=1).astype(BF16),
          "d_skip": row(d_skip), "w_glu": w_glu[l].astype(BF16), "b_glu": row(b_glu),
          "g_ssm_out": row(g_ssm_out),
          "w_out_ssm": w_out[l][:D_SSM].astype(BF16), "w_out_att": w_out[l][D_SSM:].astype(BF16),
          "g_mix_post": row(g_mix_post), "g_ffn2_pre": row(g_ffn2_pre),
          "wgu2": jnp.concatenate([w_ffn2_gate[l], w_ffn2_up[l]], axis=1).astype(BF16),
          "wd2": w_ffn2_down[l].astype(BF16), "g_ffn2_post": row(g_ffn2_post),
      }
      x1, u, q, k, v = _pre_call(h, seq_len, tables[seq_len], w)
      ya = _attn_call(q, k, v, row(g_att_out), batch, seq_len)
      fwd = _ssm_operands(lam_re_fwd[l], lam_im_fwd[l], log_dt_fwd[l], b_re_fwd[l], b_im_fwd[l],
                          c_re_fwd[l], c_im_fwd[l])
      bwd = _ssm_operands(lam_re_bwd[l], lam_im_bwd[l], log_dt_bwd[l], b_re_bwd[l], b_im_bwd[l],
                          c_re_bwd[l], c_im_bwd[l])
      u3 = u.reshape(batch, seq_len, D_SSM)
      yf = _ssm_call(u3, *fwd, reverse=False).reshape(u.shape)
      yb = _ssm_call(u3, *bwd, reverse=True).reshape(u.shape)
      h = _post_call(x1, u, yf, yb, ya, w)
    return h.reshape(batch, seq_len, D_MODEL)

  return (trunk(x_prompt), trunk(x_sample))
```

```python
import functools
import math

import jax
import jax.numpy as jnp
from jax import lax
from jax.experimental import pallas as pl
from jax.experimental.pallas import tpu as pltpu

F32 = jnp.float32
BF16 = jnp.bfloat16

D_MODEL = 1024
D_FF = 2816
D_SSM = 512
SSM_GROUP = 16
N_SSM_GROUPS = 32
SSM_STATE = 64
N_HEADS = 8
QK_NOPE = 64
QK_ROPE = 32
V_HEAD = 64
D_ATTN = N_HEADS * V_HEAD
Q_RANK = 384
KV_RANK = 256
ROPE_THETA = 10000.0
EPS = 1e-6
LOG2E = math.log2(math.e)

LANES = 128
SUBLANES = 8
VMEM_LIMIT_BYTES = 56 * 1024 * 1024

HEAD_PAD = LANES
D_HEADS_PAD = N_HEADS * HEAD_PAD
D_IN_PAD = D_SSM + Q_RANK + KV_RANK + LANES
MXU_DIM = 256
FF_CHUNKS = ((0, 6 * MXU_DIM), (6 * MXU_DIM, D_FF))
ROW_SPLIT = 2

TM = 512
TQ = 256
TC = 64
GROUPS_PER_TILE = LANES // SSM_GROUP
N_QUARTERS = D_SSM // LANES
STATE_TILE = GROUPS_PER_TILE * SSM_STATE


def _const_spec(shape):
  zeros = (0,) * len(shape)
  return pl.BlockSpec(shape, lambda *_: zeros, pipeline_mode=pl.Buffered(1))


def _rms(x, g):
  ms = jnp.mean(x * x, axis=-1, keepdims=True)
  return x * lax.rsqrt(ms + EPS) * g


def _swiglu(h, wgu_ref, wd_ref):
  acc = None
  for lo, hi in FF_CHUNKS:
    gate = jnp.dot(h, wgu_ref[:, lo:hi], preferred_element_type=F32)
    up = jnp.dot(h, wgu_ref[:, D_FF + lo:D_FF + hi], preferred_element_type=F32)
    act = (gate * jax.nn.sigmoid(gate) * up).astype(BF16)
    part = jnp.dot(act, wd_ref[lo:hi, :], preferred_element_type=F32)
    acc = part if acc is None else acc + part
  return acc


def _pre_kernel(x_ref, cq_ref, sq_ref, ck_ref, sk_ref,
                g1pre_ref, wgu_ref, wd_ref, g1post_ref, gmix_ref, win_ref,
                gq_ref, wuq_ref, gkv_ref, wukv_ref,
                x1_ref, u_ref, q_ref, k_ref, v_ref):
  rows_per = x_ref.shape[0] // ROW_SPLIT
  half = QK_ROPE // 2
  lane = lax.broadcasted_iota(jnp.int32, (rows_per, LANES), 1)
  ones_col = (lane == V_HEAD).astype(F32)
  for part in range(ROW_SPLIT):
    rows = slice(part * rows_per, (part + 1) * rows_per)
    x = x_ref[rows, :]
    h = _rms(x, g1pre_ref[...]).astype(BF16)
    f = _swiglu(h, wgu_ref, wd_ref)
    x1 = x + 0.5 * _rms(f, g1post_ref[...])
    x1_ref[rows, :] = x1

    h2 = _rms(x1, gmix_ref[...]).astype(BF16)
    z = jnp.dot(h2, win_ref[...], preferred_element_type=F32)
    u_ref[rows, :] = z[:, :D_SSM]
    q_c = z[:, D_SSM:D_SSM + Q_RANK]
    kv_c = z[:, D_SSM + Q_RANK:D_SSM + Q_RANK + KV_RANK]
    kr = z[:, D_SSM + Q_RANK + KV_RANK:]

    kr_sw = jnp.where(lane < half, pltpu.roll(kr, LANES - half, 1), pltpu.roll(kr, half, 1))
    k_pe = pltpu.roll(kr * ck_ref[rows, :] + kr_sw * sk_ref[rows, :], QK_NOPE, 1)

    qn = _rms(q_c, gq_ref[...]).astype(BF16)
    q_raw = jnp.dot(qn, wuq_ref[...], preferred_element_type=F32)
    kvn = _rms(kv_c, gkv_ref[...]).astype(BF16)
    kv_raw = jnp.dot(kvn, wukv_ref[...], preferred_element_type=F32)

    cq = cq_ref[rows, :]
    sq = sq_ref[rows, :]
    for hd in range(N_HEADS):
      lo = hd * HEAD_PAD
      t = q_raw[:, lo:lo + HEAD_PAD]
      t_sw = jnp.where(lane < QK_NOPE + half, pltpu.roll(t, LANES - half, 1), pltpu.roll(t, half, 1))
      q_ref[rows, lo:lo + HEAD_PAD] = (t * cq + t_sw * sq).astype(BF16)
      k_ref[rows, lo:lo + HEAD_PAD] = (kv_raw[:, lo:lo + HEAD_PAD] + k_pe).astype(BF16)
      v_ref[rows, lo:lo + HEAD_PAD] = (
          kv_raw[:, D_HEADS_PAD + lo:D_HEADS_PAD + lo + HEAD_PAD] + ones_col).astype(BF16)


def _pre_call(x2d, seq_len, tabs, w):
  n = x2d.shape[0]
  nlt = seq_len // TM
  row = lambda i: (i, 0)
  pos = lambda i: (i % nlt, 0)
  tab_spec = pl.BlockSpec((TM, LANES), pos)
  in_specs = [
      pl.BlockSpec((TM, D_MODEL), row), tab_spec, tab_spec, tab_spec, tab_spec,
      _const_spec((1, D_MODEL)), _const_spec((D_MODEL, 2 * D_FF)), _const_spec((D_FF, D_MODEL)),
      _const_spec((1, D_MODEL)), _const_spec((1, D_MODEL)), _const_spec((D_MODEL, D_IN_PAD)),
      _const_spec((1, Q_RANK)), _const_spec((Q_RANK, D_HEADS_PAD)),
      _const_spec((1, KV_RANK)), _const_spec((KV_RANK, 2 * D_HEADS_PAD)),
  ]
  out_shape = [
      jax.ShapeDtypeStruct((n, D_MODEL), F32), jax.ShapeDtypeStruct((n, D_SSM), F32),
      jax.ShapeDtypeStruct((n, D_HEADS_PAD), BF16), jax.ShapeDtypeStruct((n, D_HEADS_PAD), BF16),
      jax.ShapeDtypeStruct((n, D_HEADS_PAD), BF16),
  ]
  out_specs = [
      pl.BlockSpec((TM, D_MODEL), row), pl.BlockSpec((TM, D_SSM), row),
      pl.BlockSpec((TM, D_HEADS_PAD), row), pl.BlockSpec((TM, D_HEADS_PAD), row),
      pl.BlockSpec((TM, D_HEADS_PAD), row),
  ]
  return pl.pallas_call(
      _pre_kernel, grid=(n // TM,), in_specs=in_specs, out_specs=out_specs, out_shape=out_shape,
      compiler_params=pltpu.CompilerParams(
          dimension_semantics=("arbitrary",), vmem_limit_bytes=VMEM_LIMIT_BYTES),
      name="pre",
  )(x2d, tabs["cq"], tabs["sq"], tabs["ck"], tabs["sk"],
    w["g_ffn1_pre"], w["wgu1"], w["wd1"], w["g_ffn1_post"], w["g_mix_pre"], w["w_in"],
    w["g_q"], w["w_uq"], w["g_kv"], w["w_ukv"])


def _attn_kernel(q_ref, k_ref, v_ref, g_ref, o_ref, s_ref):
  tq = q_ref.shape[0]
  lane = lax.broadcasted_iota(jnp.int32, (tq, HEAD_PAD), 1)

  def scores(hd):
    lo = hd * HEAD_PAD
    s = lax.dot_general(q_ref[:, lo:lo + HEAD_PAD], k_ref[:, lo:lo + HEAD_PAD],
                        (((1,), (1,)), ((), ())), preferred_element_type=F32)
    s_ref[hd % 2] = s
    return jnp.max(s, axis=-1, keepdims=True)

  heads = []
  ssq = jnp.zeros((tq, 1), F32)
  m_next = scores(0)
  for hd in range(N_HEADS):
    lo = hd * HEAD_PAD
    m = m_next
    if hd + 1 < N_HEADS:
      m_next = scores(hd + 1)
    p = jnp.exp2(s_ref[hd % 2] - m).astype(BF16)
    oa = jnp.dot(p, v_ref[:, lo:lo + HEAD_PAD], preferred_element_type=F32)
    o = jnp.where(lane < V_HEAD, oa / oa[:, V_HEAD:V_HEAD + 1], 0.0)
    ssq = ssq + jnp.sum(o * o, axis=-1, keepdims=True)
    heads.append(o)
  inv = lax.rsqrt(ssq * (1.0 / D_ATTN) + EPS)
  for j in range(N_HEADS // 2):
    pair = heads[2 * j] + pltpu.roll(heads[2 * j + 1], V_HEAD, 1)
    o_ref[:, j * LANES:(j + 1) * LANES] = (
        pair * inv * g_ref[:, j * LANES:(j + 1) * LANES]).astype(BF16)


def _attn_call(q, k, v, g_att, batch, seq_len):
  n = q.shape[0]
  nqt = seq_len // TQ
  kv_spec = pl.BlockSpec((seq_len, D_HEADS_PAD), lambda b, i: (b, 0), pipeline_mode=pl.Buffered(1))
  return pl.pallas_call(
      _attn_kernel, grid=(batch, nqt),
      in_specs=[pl.BlockSpec((TQ, D_HEADS_PAD), lambda b, i: (b * nqt + i, 0)), kv_spec, kv_spec,
                _const_spec((1, D_ATTN))],
      out_specs=pl.BlockSpec((TQ, D_ATTN), lambda b, i: (b * nqt + i, 0)),
      out_shape=jax.ShapeDtypeStruct((n, D_ATTN), BF16),
      scratch_shapes=[pltpu.VMEM((2, TQ, seq_len), F32)],
      compiler_params=pltpu.CompilerParams(
          dimension_semantics=("arbitrary", "arbitrary"), vmem_limit_bytes=VMEM_LIMIT_BYTES),
      name="attn",
  )(q, k, v, g_att)


def _disc_kernel(lre_ref, lim_ref, ldt_ref, bre_ref, bim_ref, are_ref, aim_ref, bbre_ref, bbim_ref):
  lre = lre_ref[...]
  lim = lim_ref[...]
  dt = jnp.exp(ldt_ref[...])
  mag = jnp.exp(lre * dt)
  ar = mag * jnp.cos(lim * dt)
  ai = mag * jnp.sin(lim * dt)
  are_ref[...] = ar
  aim_ref[...] = ai
  den = lre * lre + lim * lim
  cr = ((ar - 1.0) * lre + ai * lim) / den
  ci = (ai * lre - (ar - 1.0) * lim) / den
  bre = bre_ref[...]
  bim = bim_ref[...]
  bbre_ref[...] = cr * bre - ci * bim
  bbim_ref[...] = cr * bim + ci * bre


def _discretise(lam_re, lam_im, log_dt, b_re_t, b_im_t):
  n, c, p = b_re_t.shape
  return pl.pallas_call(
      _disc_kernel,
      out_shape=[jax.ShapeDtypeStruct((n, 1, p), F32), jax.ShapeDtypeStruct((n, 1, p), F32),
                 jax.ShapeDtypeStruct((n, c, p), F32), jax.ShapeDtypeStruct((n, c, p), F32)],
      name="s5_discretise",
  )(lam_re, lam_im, log_dt, b_re_t, b_im_t)


def _ssm_kernel(*refs, tc, reverse):
  nb = SUBLANES
  u_ref, bmat_ref, cre_ref, cim_ref, lam_ref, y_ref, ut_ref, st_ref, yt_ref, carry_ref = refs

  @pl.when(pl.program_id(0) == 0)
  def _():
    carry_ref[...] = jnp.zeros_like(carry_ref)

  for b in range(nb):
    for q in range(N_QUARTERS):
      ut_ref[q, pl.ds(b, tc, stride=nb), :] = u_ref[b, :, q * LANES:(q + 1) * LANES]
  re = slice(0, STATE_TILE)
  im = slice(STATE_TILE, 2 * STATE_TILE)
  for q in range(N_QUARTERS):
    st_ref[q] = jnp.dot(ut_ref[q].astype(BF16), bmat_ref[q], preferred_element_type=F32)

    a_re = lam_ref[0, :, q * STATE_TILE:(q + 1) * STATE_TILE]
    a_im = lam_ref[1, :, q * STATE_TILE:(q + 1) * STATE_TILE]
    s_re = carry_ref[q, :, re]
    s_im = carry_ref[q, :, im]
    for k in range(tc):
      r = ((tc - 1 - k) if reverse else k) * nb
      n_re = a_re * s_re - a_im * s_im + st_ref[q, r:r + nb, re]
      n_im = a_re * s_im + a_im * s_re + st_ref[q, r:r + nb, im]
      st_ref[q, r:r + nb, re] = n_re
      st_ref[q, r:r + nb, im] = n_im
      s_re, s_im = n_re, n_im
    carry_ref[q, :, re] = s_re
    carry_ref[q, :, im] = s_im

    yt_ref[q] = (
        jnp.dot(st_ref[q, :, re].astype(BF16), cre_ref[q], preferred_element_type=F32)
        - jnp.dot(st_ref[q, :, im].astype(BF16), cim_ref[q], preferred_element_type=F32))
  for b in range(nb):
    for q in range(N_QUARTERS):
      y_ref[b, :, q * LANES:(q + 1) * LANES] = yt_ref[q, pl.ds(b, tc, stride=nb), :]


def _ssm_call(u, bmat, cre, cim, lam, reverse):
  nb, seq_len, _ = u.shape
  assert nb == SUBLANES
  nblk = seq_len // TC
  blk = (lambda i: (0, nblk - 1 - i, 0)) if reverse else (lambda i: (0, i, 0))
  seq_spec = pl.BlockSpec((nb, TC, D_SSM), blk)
  in_specs = [seq_spec] + [
      _const_spec((N_QUARTERS, LANES, 2 * STATE_TILE)),
      _const_spec((N_QUARTERS, STATE_TILE, LANES)),
      _const_spec((N_QUARTERS, STATE_TILE, LANES)),
      _const_spec((2, nb, N_QUARTERS * STATE_TILE)),
  ]
  rows = TC * nb
  return pl.pallas_call(
      functools.partial(_ssm_kernel, tc=TC, reverse=reverse), grid=(nblk,),
      in_specs=in_specs, out_specs=seq_spec,
      out_shape=jax.ShapeDtypeStruct(u.shape, F32),
      scratch_shapes=[pltpu.VMEM((N_QUARTERS, rows, LANES), F32),
                      pltpu.VMEM((N_QUARTERS, rows, 2 * STATE_TILE), F32),
                      pltpu.VMEM((N_QUARTERS, rows, LANES), F32),
                      pltpu.VMEM((N_QUARTERS, nb, 2 * STATE_TILE), F32)],
      compiler_params=pltpu.CompilerParams(
          dimension_semantics=("arbitrary",), vmem_limit_bytes=VMEM_LIMIT_BYTES),
      name="ssm_bwd" if reverse else "ssm_fwd",
  )(u, bmat, cre, cim, lam)


def _post_kernel(x1_ref, u_ref, yf_ref, yb_ref, ya_ref,
                 dskip_ref, wglu_ref, bglu_ref, gssm_ref, wos_ref, woa_ref, gpost_ref,
                 g2pre_ref, wgu_ref, wd_ref, g2post_ref, o_ref):
  rows_per = x1_ref.shape[0] // ROW_SPLIT
  for part in range(ROW_SPLIT):
    rows = slice(part * rows_per, (part + 1) * rows_per)
    y = yf_ref[rows, :] + yb_ref[rows, :] + dskip_ref[...] * u_ref[rows, :]
    y = y * (0.5 * (1.0 + jnp.tanh(math.sqrt(2.0 / math.pi) * (y + 0.044715 * (y * y * y)))))
    gate = jnp.dot(y.astype(BF16), wglu_ref[...], preferred_element_type=F32) + bglu_ref[...]
    y = y * jax.nn.sigmoid(gate)
    y_ssm = _rms(y, gssm_ref[...]).astype(BF16)
    m = (jnp.dot(y_ssm, wos_ref[...], preferred_element_type=F32)
         + jnp.dot(ya_ref[rows, :], woa_ref[...], preferred_element_type=F32))
    x2 = x1_ref[rows, :] + _rms(m, gpost_ref[...])
    h = _rms(x2, g2pre_ref[...]).astype(BF16)
    f = _swiglu(h, wgu_ref, wd_ref)
    o_ref[rows, :] = x2 + 0.5 * _rms(f, g2post_ref[...])


def _post_call(x1, u, yf, yb, ya, w):
  n = x1.shape[0]
  row = lambda i: (i, 0)
  in_specs = [
      pl.BlockSpec((TM, D_MODEL), row), pl.BlockSpec((TM, D_SSM), row),
      pl.BlockSpec((TM, D_SSM), row), pl.BlockSpec((TM, D_SSM), row), pl.BlockSpec((TM, D_ATTN), row),
      _const_spec((1, D_SSM)), _const_spec((D_SSM, D_SSM)), _const_spec((1, D_SSM)),
      _const_spec((1, D_SSM)), _const_spec((D_SSM, D_MODEL)), _const_spec((D_ATTN, D_MODEL)),
      _const_spec((1, D_MODEL)), _const_spec((1, D_MODEL)),
      _const_spec((D_MODEL, 2 * D_FF)), _const_spec((D_FF, D_MODEL)), _const_spec((1, D_MODEL)),
  ]
  return pl.pallas_call(
      _post_kernel, grid=(n // TM,), in_specs=in_specs,
      out_specs=pl.BlockSpec((TM, D_MODEL), row),
      out_shape=jax.ShapeDtypeStruct((n, D_MODEL), F32),
      compiler_params=pltpu.CompilerParams(
          dimension_semantics=("arbitrary",), vmem_limit_bytes=VMEM_LIMIT_BYTES),
      name="post",
  )(x1, u, yf, yb, ya,
    w["d_skip"], w["w_glu"], w["b_glu"], w["g_ssm_out"], w["w_out_ssm"], w["w_out_att"],
    w["g_mix_post"], w["g_ffn2_pre"], w["wgu2"], w["wd2"], w["g_ffn2_post"])


def _rope_tables(length):
  inv = 1.0 / (ROPE_THETA ** (jnp.arange(0, QK_ROPE, 2, dtype=F32) / QK_ROPE))
  ang = jnp.arange(length, dtype=F32)[:, None] * inv[None, :]
  cos, sin = jnp.cos(ang), jnp.sin(ang)
  half = QK_ROPE // 2
  zk = jnp.zeros((length, LANES - QK_ROPE), F32)
  ck = jnp.concatenate([cos, cos, zk], axis=1)
  sk = jnp.concatenate([-sin, sin, zk], axis=1)
  qs = (QK_NOPE + QK_ROPE) ** -0.5 * LOG2E
  zq = jnp.zeros((length, LANES - QK_NOPE - QK_ROPE), F32)
  cq = qs * jnp.concatenate([jnp.ones((length, QK_NOPE), F32), cos, cos, zq], axis=1)
  sq = qs * jnp.concatenate([jnp.zeros((length, QK_NOPE), F32), -sin, sin, zq], axis=1)
  del half
  return {"cq": cq, "sq": sq, "ck": ck, "sk": sk}


def _head_tiles(wmat, width):
  k = wmat.shape[0]
  wh = wmat.reshape(k, N_HEADS, width)
  return jnp.pad(wh, ((0, 0), (0, 0), (0, HEAD_PAD - width))).reshape(k, D_HEADS_PAD)


def _block_diag_tiles(m):
  eye = jnp.eye(GROUPS_PER_TILE, dtype=m.dtype)
  out = m[:, :, :, None, :] * eye[None, :, None, :, None]
  return out.reshape(N_QUARTERS, GROUPS_PER_TILE * m.shape[2], GROUPS_PER_TILE * m.shape[3])


def _ssm_operands(lam_re, lam_im, log_dt, b_re, b_im, c_re, c_im):
  g, p, c = N_SSM_GROUPS, SSM_STATE, SSM_GROUP
  a_re, a_im, bb_re, bb_im = _discretise(
      lam_re.reshape(g, 1, p), lam_im.reshape(g, 1, p),
      jnp.broadcast_to(log_dt.reshape(g, 1, 1), (g, 1, p)),
      jnp.swapaxes(b_re, 1, 2), jnp.swapaxes(b_im, 1, 2))
  q4 = lambda m: m.reshape(N_QUARTERS, GROUPS_PER_TILE, *m.shape[1:])
  bmat = jnp.concatenate([_block_diag_tiles(q4(bb_re)), _block_diag_tiles(q4(bb_im))], axis=2)
  cre = _block_diag_tiles(q4(jnp.swapaxes(c_re, 1, 2)))
  cim = _block_diag_tiles(q4(jnp.swapaxes(c_im, 1, 2)))
  lam = jnp.stack([a_re.reshape(1, g * p), a_im.reshape(1, g * p)])
  lam = jnp.broadcast_to(lam, (2, SUBLANES, g * p))
  return bmat.astype(BF16), cre.astype(BF16), cim.astype(BF16), lam


def kernel(x_prompt, x_sample, g_ffn1_pre, w_ffn1_gate, w_ffn1_up, w_ffn1_down, g_ffn1_post, g_mix_pre, w_in, lam_re_fwd, lam_im_fwd, log_dt_fwd, b_re_fwd, b_im_fwd, c_re_fwd, c_im_fwd, lam_re_bwd, lam_im_bwd, log_dt_bwd, b_re_bwd, b_im_bwd, c_re_bwd, c_im_bwd, d_skip, w_glu, b_glu, g_ssm_out, g_q, w_uq, g_kv, w_ukv, g_att_out, w_out, g_mix_post, g_ffn2_pre, w_ffn2_gate, w_ffn2_up, w_ffn2_down, g_ffn2_post):
  depth = w_in.shape[0]
  tables = {}

  def trunk(x):
    batch, seq_len, _ = x.shape
    assert batch == SUBLANES and seq_len % TM == 0 and seq_len % TQ == 0 and seq_len % TC == 0
    if seq_len not in tables:
      tables[seq_len] = _rope_tables(seq_len)
    h = x.reshape(batch * seq_len, D_MODEL)
    for l in range(depth):
      row = lambda v: v[l].reshape(1, -1)
      win = w_in[l]
      win_pad = jnp.concatenate(
          [win, jnp.zeros((D_MODEL, D_IN_PAD - win.shape[1]), F32)], axis=1).astype(BF16)
      wukv = w_ukv[l].reshape(KV_RANK, N_HEADS, QK_NOPE + V_HEAD)
      w = {
          "g_ffn1_pre": row(g_ffn1_pre),
          "wgu1": jnp.concatenate([w_ffn1_gate[l], w_ffn1_up[l]], axis=1).astype(BF16),
          "wd1": w_ffn1_down[l].astype(BF16), "g_ffn1_post": row(g_ffn1_post),
          "g_mix_pre": row(g_mix_pre), "w_in": win_pad,
          "g_q": row(g_q), "w_uq": _head_tiles(w_uq[l], QK_NOPE + QK_ROPE).astype(BF16),
          "g_kv": row(g_kv),
          "w_ukv": jnp.concatenate(
              [_head_tiles(wukv[:, :, :QK_NOPE].reshape(KV_RANK, -1), QK_NOPE),
               _head_tiles(wukv[:, :, QK_NOPE:].reshape(KV_RANK, -1), V_HEAD)], axis=1).astype(BF16),
          "d_skip": row(d_skip), "w_glu": w_glu[l].astype(BF16), "b_glu": row(b_glu),
          "g_ssm_out": row(g_ssm_out),
          "w_out_ssm": w_out[l][:D_SSM].astype(BF16), "w_out_att": w_out[l][D_SSM:].astype(BF16),
          "g_mix_post": row(g_mix_post), "g_ffn2_pre": row(g_ffn2_pre),
          "wgu2": jnp.concatenate([w_ffn2_gate[l], w_ffn2_up[l]], axis=1).astype(BF16),
          "wd2": w_ffn2_down[l].astype(BF16), "g_ffn2_post": row(g_ffn2_post),
      }
      x1, u, q, k, v = _pre_call(h, seq_len, tables[seq_len], w)
      ya = _attn_call(q, k, v, row(g_att_out), batch, seq_len)
      fwd = _ssm_operands(lam_re_fwd[l], lam_im_fwd[l], log_dt_fwd[l], b_re_fwd[l], b_im_fwd[l],
                          c_re_fwd[l], c_im_fwd[l])
      bwd = _ssm_operands(lam_re_bwd[l], lam_im_bwd[l], log_dt_bwd[l], b_re_bwd[l], b_im_bwd[l],
                          c_re_bwd[l], c_im_bwd[l])
      u3 = u.reshape(batch, seq_len, D_SSM)
      yf = _ssm_call(u3, *fwd, reverse=False).reshape(u.shape)
      yb = _ssm_call(u3, *bwd, reverse=True).reshape(u.shape)
      h = _post_call(x1, u, yf, yb, ya, w)
    return h.reshape(batch, seq_len, D_MODEL)

  return (trunk(x_prompt), trunk(x_sample))
```

```python
import functools
import math

import jax
import jax.numpy as jnp
from jax import lax
from jax.experimental import pallas as pl
from jax.experimental.pallas import tpu as pltpu

F32 = jnp.float32
BF16 = jnp.bfloat16

D_MODEL = 1024
D_FF = 2816
D_SSM = 512
SSM_GROUP = 16
N_SSM_GROUPS = 32
SSM_STATE = 64
N_HEADS = 8
QK_NOPE = 64
QK_ROPE = 32
V_HEAD = 64
D_ATTN = N_HEADS * V_HEAD
Q_RANK = 384
KV_RANK = 256
ROPE_THETA = 10000.0
EPS = 1e-6
LOG2E = math.log2(math.e)

LANES = 128
SUBLANES = 8
VMEM_LIMIT_BYTES = 56 * 1024 * 1024

HEAD_PAD = LANES
D_HEADS_PAD = N_HEADS * HEAD_PAD
D_IN_PAD = D_SSM + Q_RANK + KV_RANK + LANES
MXU_DIM = 256
FF_CHUNKS = ((0, 6 * MXU_DIM), (6 * MXU_DIM, D_FF))
ROW_SPLIT = 2

TM = 512
TQ = 256
GROUPS_PER_TILE = LANES // SSM_GROUP
N_QUARTERS = D_SSM // LANES
CHUNK = 16
HALF = CHUNK // 2
G_LANES = CHUNK * SSM_GROUP
TB = 512
SCAN_W = 512


def _const_spec(shape):
  zeros = (0,) * len(shape)
  return pl.BlockSpec(shape, lambda *_: zeros, pipeline_mode=pl.Buffered(1))


def _rms(x, g):
  ms = jnp.mean(x * x, axis=-1, keepdims=True)
  return x * lax.rsqrt(ms + EPS) * g


def _swiglu(h, wgu_ref, wd_ref):
  acc = None
  for lo, hi in FF_CHUNKS:
    gate = jnp.dot(h, wgu_ref[:, lo:hi], preferred_element_type=F32)
    up = jnp.dot(h, wgu_ref[:, D_FF + lo:D_FF + hi], preferred_element_type=F32)
    act = (gate * jax.nn.sigmoid(gate) * up).astype(BF16)
    part = jnp.dot(act, wd_ref[lo:hi, :], preferred_element_type=F32)
    acc = part if acc is None else acc + part
  return acc


def _pre_kernel(x_ref, cq_ref, sq_ref, ck_ref, sk_ref,
                g1pre_ref, wgu_ref, wd_ref, g1post_ref, gmix_ref, win_ref,
                gq_ref, wuq_ref, gkv_ref, wukv_ref,
                x1_ref, u_ref, q_ref, k_ref, v_ref):
  rows_per = x_ref.shape[0] // ROW_SPLIT
  half = QK_ROPE // 2
  lane = lax.broadcasted_iota(jnp.int32, (rows_per, LANES), 1)
  ones_col = (lane == V_HEAD).astype(F32)
  for part in range(ROW_SPLIT):
    rows = slice(part * rows_per, (part + 1) * rows_per)
    x = x_ref[rows, :]
    h = _rms(x, g1pre_ref[...]).astype(BF16)
    f = _swiglu(h, wgu_ref, wd_ref)
    x1 = x + 0.5 * _rms(f, g1post_ref[...])
    x1_ref[rows, :] = x1

    h2 = _rms(x1, gmix_ref[...]).astype(BF16)
    z = jnp.dot(h2, win_ref[...], preferred_element_type=F32)
    u_ref[rows, :] = z[:, :D_SSM]
    q_c = z[:, D_SSM:D_SSM + Q_RANK]
    kv_c = z[:, D_SSM + Q_RANK:D_SSM + Q_RANK + KV_RANK]
    kr = z[:, D_SSM + Q_RANK + KV_RANK:]

    kr_sw = jnp.where(lane < half, pltpu.roll(kr, LANES - half, 1), pltpu.roll(kr, half, 1))
    k_pe = pltpu.roll(kr * ck_ref[rows, :] + kr_sw * sk_ref[rows, :], QK_NOPE, 1)

    qn = _rms(q_c, gq_ref[...]).astype(BF16)
    q_raw = jnp.dot(qn, wuq_ref[...], preferred_element_type=F32)
    kvn = _rms(kv_c, gkv_ref[...]).astype(BF16)
    kv_raw = jnp.dot(kvn, wukv_ref[...], preferred_element_type=F32)

    cq = cq_ref[rows, :]
    sq = sq_ref[rows, :]
    for hd in range(N_HEADS):
      lo = hd * HEAD_PAD
      t = q_raw[:, lo:lo + HEAD_PAD]
      t_sw = jnp.where(lane < QK_NOPE + half, pltpu.roll(t, LANES - half, 1), pltpu.roll(t, half, 1))
      q_ref[rows, lo:lo + HEAD_PAD] = (t * cq + t_sw * sq).astype(BF16)
      k_ref[rows, lo:lo + HEAD_PAD] = (kv_raw[:, lo:lo + HEAD_PAD] + k_pe).astype(BF16)
      v_ref[rows, lo:lo + HEAD_PAD] = (
          kv_raw[:, D_HEADS_PAD + lo:D_HEADS_PAD + lo + HEAD_PAD] + ones_col).astype(BF16)


def _pre_call(x2d, seq_len, tabs, w):
  n = x2d.shape[0]
  nlt = seq_len // TM
  row = lambda i: (i, 0)
  pos = lambda i: (i % nlt, 0)
  tab_spec = pl.BlockSpec((TM, LANES), pos)
  in_specs = [
      pl.BlockSpec((TM, D_MODEL), row), tab_spec, tab_spec, tab_spec, tab_spec,
      _const_spec((1, D_MODEL)), _const_spec((D_MODEL, 2 * D_FF)), _const_spec((D_FF, D_MODEL)),
      _const_spec((1, D_MODEL)), _const_spec((1, D_MODEL)), _const_spec((D_MODEL, D_IN_PAD)),
      _const_spec((1, Q_RANK)), _const_spec((Q_RANK, D_HEADS_PAD)),
      _const_spec((1, KV_RANK)), _const_spec((KV_RANK, 2 * D_HEADS_PAD)),
  ]
  out_shape = [
      jax.ShapeDtypeStruct((n, D_MODEL), F32), jax.ShapeDtypeStruct((n, D_SSM), F32),
      jax.ShapeDtypeStruct((n, D_HEADS_PAD), BF16), jax.ShapeDtypeStruct((n, D_HEADS_PAD), BF16),
      jax.ShapeDtypeStruct((n, D_HEADS_PAD), BF16),
  ]
  out_specs = [
      pl.BlockSpec((TM, D_MODEL), row), pl.BlockSpec((TM, D_SSM), row),
      pl.BlockSpec((TM, D_HEADS_PAD), row), pl.BlockSpec((TM, D_HEADS_PAD), row),
      pl.BlockSpec((TM, D_HEADS_PAD), row),
  ]
  return pl.pallas_call(
      _pre_kernel, grid=(n // TM,), in_specs=in_specs, out_specs=out_specs, out_shape=out_shape,
      compiler_params=pltpu.CompilerParams(
          dimension_semantics=("arbitrary",), vmem_limit_bytes=VMEM_LIMIT_BYTES),
      name="pre",
  )(x2d, tabs["cq"], tabs["sq"], tabs["ck"], tabs["sk"],
    w["g_ffn1_pre"], w["wgu1"], w["wd1"], w["g_ffn1_post"], w["g_mix_pre"], w["w_in"],
    w["g_q"], w["w_uq"], w["g_kv"], w["w_ukv"])


def _attn_kernel(q_ref, k_ref, v_ref, g_ref, o_ref, s_ref):
  tq = q_ref.shape[0]
  lane = lax.broadcasted_iota(jnp.int32, (tq, HEAD_PAD), 1)

  def scores(hd):
    lo = hd * HEAD_PAD
    s = lax.dot_general(q_ref[:, lo:lo + HEAD_PAD], k_ref[:, lo:lo + HEAD_PAD],
                        (((1,), (1,)), ((), ())), preferred_element_type=F32)
    s_ref[hd % 2] = s
    return jnp.max(s, axis=-1, keepdims=True)

  heads = []
  ssq = jnp.zeros((tq, 1), F32)
  m_next = scores(0)
  for hd in range(N_HEADS):
    lo = hd * HEAD_PAD
    m = m_next
    if hd + 1 < N_HEADS:
      m_next = scores(hd + 1)
    p = jnp.exp2(s_ref[hd % 2] - m).astype(BF16)
    oa = jnp.dot(p, v_ref[:, lo:lo + HEAD_PAD], preferred_element_type=F32)
    o = jnp.where(lane < V_HEAD, oa / oa[:, V_HEAD:V_HEAD + 1], 0.0)
    ssq = ssq + jnp.sum(o * o, axis=-1, keepdims=True)
    heads.append(o)
  inv = lax.rsqrt(ssq * (1.0 / D_ATTN) + EPS)
  for j in range(N_HEADS // 2):
    pair = heads[2 * j] + pltpu.roll(heads[2 * j + 1], V_HEAD, 1)
    o_ref[:, j * LANES:(j + 1) * LANES] = (
        pair * inv * g_ref[:, j * LANES:(j + 1) * LANES]).astype(BF16)


def _attn_call(q, k, v, g_att, batch, seq_len):
  n = q.shape[0]
  nqt = seq_len // TQ
  kv_spec = pl.BlockSpec((seq_len, D_HEADS_PAD), lambda b, i: (b, 0), pipeline_mode=pl.Buffered(1))
  return pl.pallas_call(
      _attn_kernel, grid=(batch, nqt),
      in_specs=[pl.BlockSpec((TQ, D_HEADS_PAD), lambda b, i: (b * nqt + i, 0)), kv_spec, kv_spec,
                _const_spec((1, D_ATTN))],
      out_specs=pl.BlockSpec((TQ, D_ATTN), lambda b, i: (b * nqt + i, 0)),
      out_shape=jax.ShapeDtypeStruct((n, D_ATTN), BF16),
      scratch_shapes=[pltpu.VMEM((2, TQ, seq_len), F32)],
      compiler_params=pltpu.CompilerParams(
          dimension_semantics=("arbitrary", "arbitrary"), vmem_limit_bytes=VMEM_LIMIT_BYTES),
      name="attn",
  )(q, k, v, g_att)


def _powers(n, xr, xi):
  mag = jnp.exp(n * xr)
  return mag * jnp.cos(n * xi), mag * jnp.sin(n * xi)


def _s5_gen_kernel(lrow_ref, bt_ref, crep_ref, c_ref, tz_ref, bx_ref, cs_ref, a_ref):
  ch, c, p = CHUNK, SSM_GROUP, SSM_STATE
  sub = lax.broadcasted_iota(jnp.int32, (ch, ch * c), 0)
  lane_i = lax.broadcasted_iota(jnp.int32, (ch, ch * c), 1) & (ch - 1)
  n_tab = lax.broadcasted_iota(jnp.int32, (2 * ch, p), 0).astype(F32)

  def each_row_repeated(tab, idx):
    return jnp.concatenate([jnp.broadcast_to(tab[i:i + 1, :], (c, p)) for i in idx], axis=0)

  def tiled(tab):
    return jnp.concatenate([tab[:ch, :]] * c, axis=0)

  tz = jnp.zeros((ch * c, ch * c), F32)
  for d in range(2):
    lre, lim = lrow_ref[d, 0:1, :], lrow_ref[d, 1:2, :]
    dt = jnp.exp(lrow_ref[d, 2:3, :])
    xr, xi = lre * dt, lim * dt
    pw_r, pw_i = _powers(n_tab, xr, xi)
    ar, ai = pw_r[1:2, :], pw_i[1:2, :]
    den = lre * lre + lim * lim
    cr = ((ar - 1.0) * lre + ai * lim) / den
    ci = (ai * lre - (ar - 1.0) * lim) / den
    btr, bti = bt_ref[d, 0], bt_ref[d, 1]
    bbr = cr * btr - ci * bti
    bbi = cr * bti + ci * btr

    idx = [ch - 1 - j for j in range(ch)] if d == 0 else list(range(ch))
    pr, pi = each_row_repeated(pw_r, idx), each_row_repeated(pw_i, idx)
    tr, ti = tiled(bbr), tiled(bbi)
    bx_ref[d] = pr * tr - pi * ti
    bx_ref[2 + d] = pr * ti + pi * tr

    idx = [i + 1 for i in range(ch)] if d == 0 else [ch - i for i in range(ch)]
    pr, pi = each_row_repeated(pw_r, idx), each_row_repeated(pw_i, idx)
    tr, ti = tiled(c_ref[d, 0]), tiled(c_ref[d, 1])
    cs_ref[d] = tr * pr - ti * pi
    cs_ref[2 + d] = -(tr * pi + ti * pr)

    pr, pi = tiled(pw_r), tiled(pw_i)
    rr = each_row_repeated(bbr, range(c))
    ri = each_row_repeated(bbi, range(c))
    wr = pr * rr - pi * ri
    wi = pr * ri + pi * rr
    lag = (jnp.dot(wr, crep_ref[d, 0], precision=lax.Precision.HIGHEST, preferred_element_type=F32)
           - jnp.dot(wi, crep_ref[d, 1], precision=lax.Precision.HIGHEST, preferred_element_type=F32))
    diff = (lane_i - sub) if d == 0 else (sub - lane_i)
    is_lag = [diff == e for e in range(ch)]
    blocks = []
    for k in range(c):
      blk = jnp.zeros((ch, ch * c), F32)
      for e in range(ch):
        blk = jnp.where(is_lag[e], lag[k * ch + e:k * ch + e + 1, :], blk)
      blocks.append(blk)
    tz = tz + jnp.concatenate(blocks, axis=0)

    a_ref[d] = pw_r[ch:ch + 1, :]
    a_ref[2 + d] = pw_i[ch:ch + 1, :]
  tz_ref[...] = tz


def _s5_operators(fwd, bwd):
  g, p, c, ch = N_SSM_GROUPS, SSM_STATE, SSM_GROUP, CHUNK
  both = lambda i: jnp.stack([fwd[i], bwd[i]], axis=1)
  lam = jnp.stack([both(0), both(1), jnp.broadcast_to(both(2)[..., None], (g, 2, p))], axis=2)
  bt = jnp.stack([jnp.swapaxes(both(3), 2, 3), jnp.swapaxes(both(4), 2, 3)], axis=2)
  cc = jnp.stack([both(5), both(6)], axis=2)
  crep = jnp.repeat(jnp.swapaxes(cc, 3, 4), ch, axis=-1)
  spec = lambda *s: pl.BlockSpec((None,) + s, lambda i: (i,) + (0,) * len(s))
  tz, bx, cs, a = pl.pallas_call(
      _s5_gen_kernel, grid=(g,),
      in_specs=[spec(2, 3, p), spec(2, 2, c, p), spec(2, 2, p, ch * c), spec(2, 2, c, p)],
      out_specs=[spec(ch * c, ch * c), spec(4, ch * c, p), spec(4, ch * c, p), spec(4, 1, p)],
      out_shape=[jax.ShapeDtypeStruct((g, ch * c, ch * c), F32),
                 jax.ShapeDtypeStruct((g, 4, ch * c, p), F32),
                 jax.ShapeDtypeStruct((g, 4, ch * c, p), F32),
                 jax.ShapeDtypeStruct((g, 4, 1, p), F32)],
      compiler_params=pltpu.CompilerParams(dimension_semantics=("arbitrary",)),
      name="s5_operators",
  )(lam, bt, crep, cc)
  tz = tz.reshape(g, c, ch, c, ch).transpose(0, 2, 1, 4, 3).reshape(g, ch * c, ch * c)
  bx = jnp.concatenate([bx[:, 0], bx[:, 1], bx[:, 2], bx[:, 3]], axis=-1)
  cs = jnp.swapaxes(jnp.concatenate([cs[:, 0], cs[:, 1], cs[:, 2], cs[:, 3]], axis=-1), 1, 2)
  a = a.reshape(g, 4, p)
  a_re = jnp.concatenate([a[:, 0], a[:, 1], a[:, 0], a[:, 1]], axis=-1).reshape(1, g * G_LANES)
  a_im = jnp.concatenate([a[:, 2], a[:, 3], a[:, 2], a[:, 3]], axis=-1).reshape(1, g * G_LANES)
  trans = jnp.broadcast_to(jnp.stack([a_re, a_im]), (2, SUBLANES, g * G_LANES))
  return tz.astype(BF16), bx.astype(BF16), cs.astype(BF16), trans


def _step_permutation():
  src = jnp.arange(HALF * LANES)
  j, gl, ci = src // LANES, (src // SSM_GROUP) % GROUPS_PER_TILE, src % SSM_GROUP
  dst = gl * LANES + j * SSM_GROUP + ci
  return (dst[:, None] == jnp.arange(HALF * LANES)[None, :]).astype(BF16)


def _s5_in_kernel(u_ref, perm_ref, bx_ref, up_ref, x_ref, ut_ref):
  nb, tb = SUBLANES, u_ref.shape[1]
  nc = tb // CHUNK
  r = nc * nb
  for b in range(nb):
    ut_ref[pl.ds(b, tb, stride=nb), :] = u_ref[b]
  halves = []
  for h in range(2):
    steps = []
    for j in range(h * HALF, (h + 1) * HALF):
      steps.append(jnp.concatenate(
          [ut_ref[(ci * CHUNK + j) * nb:(ci * CHUNK + j + 1) * nb, :] for ci in range(nc)], axis=0))
    halves.append(jnp.concatenate(steps, axis=1))
  ucat = jnp.concatenate(halves, axis=0).astype(BF16)
  uperm = jnp.dot(ucat, perm_ref[...], preferred_element_type=F32).astype(BF16)
  for gl in range(GROUPS_PER_TILE):
    ug = jnp.concatenate([uperm[h * r:(h + 1) * r, gl * LANES:(gl + 1) * LANES] for h in range(2)], axis=1)
    up_ref[:, gl * G_LANES:(gl + 1) * G_LANES] = ug
    x_ref[:, gl * G_LANES:(gl + 1) * G_LANES] = jnp.dot(ug, bx_ref[gl], preferred_element_type=F32)


def _s5_scan_kernel(x_ref, a_ref, sr_ref, sf_ref, sb_ref):
  nb = SUBLANES
  nr, w = x_ref.shape
  nc = nr // nb
  lane = lax.broadcasted_iota(jnp.int32, (nb, w), 1)
  is_fwd = (lane & (LANES - 1)) < SSM_STATE
  a_re, a_im = a_ref[0], a_ref[1]

  def body(k, s):
    rf = pl.multiple_of(k * nb, nb)
    rb = pl.multiple_of((nc - 1 - k) * nb, nb)
    sf_ref[pl.ds(rf, nb), :] = s
    sb_ref[pl.ds(rb, nb), :] = s
    x = jnp.where(is_fwd, x_ref[pl.ds(rf, nb), :], x_ref[pl.ds(rb, nb), :])
    parts = []
    for g in range(w // G_LANES):
      re = slice(g * G_LANES, g * G_LANES + LANES)
      im = slice(g * G_LANES + LANES, (g + 1) * G_LANES)
      parts.append(a_re[:, re] * s[:, re] - a_im[:, re] * s[:, im] + x[:, re])
      parts.append(a_re[:, im] * s[:, im] + a_im[:, im] * s[:, re] + x[:, im])
    return jnp.concatenate(parts, axis=1)

  lax.fori_loop(0, nc, body, jnp.zeros((nb, w), F32), unroll=4)
  row_is_fwd = (lax.broadcasted_iota(jnp.int32, (1, w), 1) & (LANES - 1)) < SSM_STATE
  sr_ref[...] = jnp.where(row_is_fwd, sf_ref[...], sb_ref[...]).astype(BF16)


def _s5_out_kernel(up_ref, sr_ref, tz_ref, cs_ref, perm_ref, y_ref, yt_ref):
  nb, tb = SUBLANES, y_ref.shape[1]
  nc = tb // CHUNK
  r = nc * nb
  ys = []
  for gl in range(GROUPS_PER_TILE):
    sl = slice(gl * G_LANES, (gl + 1) * G_LANES)
    ys.append(jnp.dot(up_ref[:, sl], tz_ref[gl], preferred_element_type=F32)
              + jnp.dot(sr_ref[:, sl], cs_ref[gl], preferred_element_type=F32))
  ycat = jnp.concatenate(
      [jnp.concatenate([y[:, h * LANES:(h + 1) * LANES] for y in ys], axis=1) for h in range(2)], axis=0)
  hi = ycat.astype(BF16)
  lo = (ycat - hi.astype(F32)).astype(BF16)
  ynat = (jnp.dot(hi, perm_ref[...], preferred_element_type=F32)
          + jnp.dot(lo, perm_ref[...], preferred_element_type=F32))
  for h in range(2):
    for j in range(HALF):
      blk = ynat[h * r:(h + 1) * r, j * LANES:(j + 1) * LANES]
      for ci in range(nc):
        t = ci * CHUNK + h * HALF + j
        yt_ref[t * nb:(t + 1) * nb, :] = blk[ci * nb:(ci + 1) * nb, :]
  for b in range(nb):
    y_ref[b] = yt_ref[pl.ds(b, tb, stride=nb), :]


def _s5_mix(u3, tz, bx, cs, trans, perm):
  nb, seq_len, _ = u3.shape
  assert nb == SUBLANES and seq_len % TB == 0
  nblk = seq_len // TB
  r = TB // CHUNK * nb
  nr = seq_len // CHUNK * nb
  width = N_SSM_GROUPS * G_LANES
  qw = GROUPS_PER_TILE * G_LANES
  seq_spec = pl.BlockSpec((nb, TB, LANES), lambda q, i: (0, i, q))
  row_spec = pl.BlockSpec((r, qw), lambda q, i: (i, q))
  op_spec = pl.BlockSpec((GROUPS_PER_TILE, G_LANES, G_LANES), lambda q, i: (q, 0, 0))
  perm_spec = _const_spec((HALF * LANES, HALF * LANES))
  params = pltpu.CompilerParams(
      dimension_semantics=("arbitrary", "arbitrary"), vmem_limit_bytes=VMEM_LIMIT_BYTES)
  up, x = pl.pallas_call(
      _s5_in_kernel, grid=(N_QUARTERS, nblk),
      in_specs=[seq_spec, perm_spec, op_spec], out_specs=[row_spec, row_spec],
      out_shape=[jax.ShapeDtypeStruct((nr, width), BF16), jax.ShapeDtypeStruct((nr, width), F32)],
      scratch_shapes=[pltpu.VMEM((TB * nb, LANES), F32)],
      compiler_params=params, name="s5_in",
  )(u3, perm, bx)
  sr = pl.pallas_call(
      _s5_scan_kernel, grid=(width // SCAN_W,),
      in_specs=[pl.BlockSpec((nr, SCAN_W), lambda j: (0, j)),
                pl.BlockSpec((2, nb, SCAN_W), lambda j: (0, 0, j))],
      out_specs=pl.BlockSpec((nr, SCAN_W), lambda j: (0, j)),
      out_shape=jax.ShapeDtypeStruct((nr, width), BF16),
      scratch_shapes=[pltpu.VMEM((nr, SCAN_W), F32), pltpu.VMEM((nr, SCAN_W), F32)],
      compiler_params=pltpu.CompilerParams(
          dimension_semantics=("arbitrary",), vmem_limit_bytes=VMEM_LIMIT_BYTES),
      name="s5_scan",
  )(x, trans)
  return pl.pallas_call(
      _s5_out_kernel, grid=(N_QUARTERS, nblk),
      in_specs=[row_spec, row_spec, op_spec, op_spec, perm_spec], out_specs=seq_spec,
      out_shape=jax.ShapeDtypeStruct(u3.shape, F32),
      scratch_shapes=[pltpu.VMEM((TB * nb, LANES), F32)],
      compiler_params=params, name="s5_out",
  )(up, sr, tz, cs, perm.T)


def _post_kernel(x1_ref, u_ref, yc_ref, ya_ref,
                 dskip_ref, wglu_ref, bglu_ref, gssm_ref, wos_ref, woa_ref, gpost_ref,
                 g2pre_ref, wgu_ref, wd_ref, g2post_ref, o_ref):
  rows_per = x1_ref.shape[0] // ROW_SPLIT
  for part in range(ROW_SPLIT):
    rows = slice(part * rows_per, (part + 1) * rows_per)
    y = yc_ref[rows, :] + dskip_ref[...] * u_ref[rows, :]
    y = y * (0.5 * (1.0 + jnp.tanh(math.sqrt(2.0 / math.pi) * (y + 0.044715 * (y * y * y)))))
    gate = jnp.dot(y.astype(BF16), wglu_ref[...], preferred_element_type=F32) + bglu_ref[...]
    y = y * jax.nn.sigmoid(gate)
    y_ssm = _rms(y, gssm_ref[...]).astype(BF16)
    m = (jnp.dot(y_ssm, wos_ref[...], preferred_element_type=F32)
         + jnp.dot(ya_ref[rows, :], woa_ref[...], preferred_element_type=F32))
    x2 = x1_ref[rows, :] + _rms(m, gpost_ref[...])
    h = _rms(x2, g2pre_ref[...]).astype(BF16)
    f = _swiglu(h, wgu_ref, wd_ref)
    o_ref[rows, :] = x2 + 0.5 * _rms(f, g2post_ref[...])


def _post_call(x1, u, yc, ya, w):
  n = x1.shape[0]
  row = lambda i: (i, 0)
  in_specs = [
      pl.BlockSpec((TM, D_MODEL), row), pl.BlockSpec((TM, D_SSM), row),
      pl.BlockSpec((TM, D_SSM), row), pl.BlockSpec((TM, D_ATTN), row),
      _const_spec((1, D_SSM)), _const_spec((D_SSM, D_SSM)), _const_spec((1, D_SSM)),
      _const_spec((1, D_SSM)), _const_spec((D_SSM, D_MODEL)), _const_spec((D_ATTN, D_MODEL)),
      _const_spec((1, D_MODEL)), _const_spec((1, D_MODEL)),
      _const_spec((D_MODEL, 2 * D_FF)), _const_spec((D_FF, D_MODEL)), _const_spec((1, D_MODEL)),
  ]
  return pl.pallas_call(
      _post_kernel, grid=(n // TM,), in_specs=in_specs,
      out_specs=pl.BlockSpec((TM, D_MODEL), row),
      out_shape=jax.ShapeDtypeStruct((n, D_MODEL), F32),
      compiler_params=pltpu.CompilerParams(
          dimension_semantics=("arbitrary",), vmem_limit_bytes=VMEM_LIMIT_BYTES),
      name="post",
  )(x1, u, yc, ya,
    w["d_skip"], w["w_glu"], w["b_glu"], w["g_ssm_out"], w["w_out_ssm"], w["w_out_att"],
    w["g_mix_post"], w["g_ffn2_pre"], w["wgu2"], w["wd2"], w["g_ffn2_post"])


def _rope_tables(length):
  inv = 1.0 / (ROPE_THETA ** (jnp.arange(0, QK_ROPE, 2, dtype=F32) / QK_ROPE))
  ang = jnp.arange(length, dtype=F32)[:, None] * inv[None, :]
  cos, sin = jnp.cos(ang), jnp.sin(ang)
  zk = jnp.zeros((length, LANES - QK_ROPE), F32)
  ck = jnp.concatenate([cos, cos, zk], axis=1)
  sk = jnp.concatenate([-sin, sin, zk], axis=1)
  qs = (QK_NOPE + QK_ROPE) ** -0.5 * LOG2E
  zq = jnp.zeros((length, LANES - QK_NOPE - QK_ROPE), F32)
  cq = qs * jnp.concatenate([jnp.ones((length, QK_NOPE), F32), cos, cos, zq], axis=1)
  sq = qs * jnp.concatenate([jnp.zeros((length, QK_NOPE), F32), -sin, sin, zq], axis=1)
  return {"cq": cq, "sq": sq, "ck": ck, "sk": sk}


def _head_tiles(wmat, width):
  k = wmat.shape[0]
  wh = wmat.reshape(k, N_HEADS, width)
  return jnp.pad(wh, ((0, 0), (0, 0), (0, HEAD_PAD - width))).reshape(k, D_HEADS_PAD)


def kernel(x_prompt, x_sample, g_ffn1_pre, w_ffn1_gate, w_ffn1_up, w_ffn1_down, g_ffn1_post, g_mix_pre, w_in, lam_re_fwd, lam_im_fwd, log_dt_fwd, b_re_fwd, b_im_fwd, c_re_fwd, c_im_fwd, lam_re_bwd, lam_im_bwd, log_dt_bwd, b_re_bwd, b_im_bwd, c_re_bwd, c_im_bwd, d_skip, w_glu, b_glu, g_ssm_out, g_q, w_uq, g_kv, w_ukv, g_att_out, w_out, g_mix_post, g_ffn2_pre, w_ffn2_gate, w_ffn2_up, w_ffn2_down, g_ffn2_post):
  depth = w_in.shape[0]
  tables = {}
  perm = _step_permutation()
  s5_ops = [
      _s5_operators(
          (lam_re_fwd[l], lam_im_fwd[l], log_dt_fwd[l], b_re_fwd[l], b_im_fwd[l], c_re_fwd[l], c_im_fwd[l]),
          (lam_re_bwd[l], lam_im_bwd[l], log_dt_bwd[l], b_re_bwd[l], b_im_bwd[l], c_re_bwd[l], c_im_bwd[l]))
      for l in range(depth)]

  def trunk(x):
    batch, seq_len, _ = x.shape
    assert batch == SUBLANES and seq_len % TM == 0 and seq_len % TQ == 0 and seq_len % TB == 0
    if seq_len not in tables:
      tables[seq_len] = _rope_tables(seq_len)
    h = x.reshape(batch * seq_len, D_MODEL)
    for l in range(depth):
      row = lambda v: v[l].reshape(1, -1)
      win = w_in[l]
      win_pad = jnp.concatenate(
          [win, jnp.zeros((D_MODEL, D_IN_PAD - win.shape[1]), F32)], axis=1).astype(BF16)
      wukv = w_ukv[l].reshape(KV_RANK, N_HEADS, QK_NOPE + V_HEAD)
      w = {
          "g_ffn1_pre": row(g_ffn1_pre),
          "wgu1": jnp.concatenate([w_ffn1_gate[l], w_ffn1_up[l]], axis=1).astype(BF16),
          "wd1": w_ffn1_down[l].astype(BF16), "g_ffn1_post": row(g_ffn1_post),
          "g_mix_pre": row(g_mix_pre), "w_in": win_pad,
          "g_q": row(g_q), "w_uq": _head_tiles(w_uq[l], QK_NOPE + QK_ROPE).astype(BF16),
          "g_kv": row(g_kv),
          "w_ukv": jnp.concatenate(
              [_head_tiles(wukv[:, :, :QK_NOPE].reshape(KV_RANK, -1), QK_NOPE),
               _head_tiles(wukv[:, :, QK_NOPE:].reshape(KV_RANK, -1), V_HEAD)], axis=1).astype(BF16),
          "d_skip": row(d_skip), "w_glu": w_glu[l].astype(BF16), "b_glu": row(b_glu),
          "g_ssm_out": row(g_ssm_out),
          "w_out_ssm": w_out[l][:D_SSM].astype(BF16), "w_out_att": w_out[l][D_SSM:].astype(BF16),
          "g_mix_post": row(g_mix_post), "g_ffn2_pre": row(g_ffn2_pre),
          "wgu2": jnp.concatenate([w_ffn2_gate[l], w_ffn2_up[l]], axis=1).astype(BF16),
          "wd2": w_ffn2_down[l].astype(BF16), "g_ffn2_post": row(g_ffn2_post),
      }
      x1, u, q, k, v = _pre_call(h, seq_len, tables[seq_len], w)
      ya = _attn_call(q, k, v, row(g_att_out), batch, seq_len)
      u3 = u.reshape(batch, seq_len, D_SSM)
      yc = _s5_mix(u3, *s5_ops[l], perm).reshape(u.shape)
      h = _post_call(x1, u, yc, ya, w)
    return h.reshape(batch, seq_len, D_MODEL)

  return (trunk(x_prompt), trunk(x_sample))
```

```python
import functools
import math

import jax
import jax.numpy as jnp
from jax import lax
from jax.experimental import pallas as pl
from jax.experimental.pallas import tpu as pltpu

F32 = jnp.float32
BF16 = jnp.bfloat16

D_MODEL = 1024
D_FF = 2816
D_SSM = 512
SSM_GROUP = 16
N_SSM_GROUPS = 32
SSM_STATE = 64
N_HEADS = 8
QK_NOPE = 64
QK_ROPE = 32
V_HEAD = 64
D_ATTN = N_HEADS * V_HEAD
Q_RANK = 384
KV_RANK = 256
ROPE_THETA = 10000.0
EPS = 1e-6
LOG2E = math.log2(math.e)

LANES = 128
SUBLANES = 8
VMEM_LIMIT_BYTES = 56 * 1024 * 1024

HEAD_PAD = LANES
D_HEADS_PAD = N_HEADS * HEAD_PAD
D_IN_PAD = D_SSM + Q_RANK + KV_RANK + LANES
MXU_DIM = 256
FF_CHUNKS = ((0, 6 * MXU_DIM), (6 * MXU_DIM, D_FF))
ROW_SPLIT = 2

TM = 512
TQ = 512
GROUPS_PER_TILE = LANES // SSM_GROUP
N_QUARTERS = D_SSM // LANES
CHUNK = 16
HALF = CHUNK // 2
G_LANES = CHUNK * SSM_GROUP
TB = 512
SCAN_W = 512


def _const_spec(shape):
  zeros = (0,) * len(shape)
  return pl.BlockSpec(shape, lambda *_: zeros, pipeline_mode=pl.Buffered(1))


def _rms(x, g):
  ms = jnp.mean(x * x, axis=-1, keepdims=True)
  return x * lax.rsqrt(ms + EPS) * g


def _swiglu(h, wgu_ref, wd_ref):
  acc = None
  for lo, hi in FF_CHUNKS:
    gate = jnp.dot(h, wgu_ref[:, lo:hi], preferred_element_type=F32)
    up = jnp.dot(h, wgu_ref[:, D_FF + lo:D_FF + hi], preferred_element_type=F32)
    act = (gate * jax.nn.sigmoid(gate) * up).astype(BF16)
    part = jnp.dot(act, wd_ref[lo:hi, :], preferred_element_type=F32)
    acc = part if acc is None else acc + part
  return acc


def _pre_kernel(x_ref, cq_ref, sq_ref, ck_ref, sk_ref,
                g1pre_ref, wgu_ref, wd_ref, g1post_ref, gmix_ref, win_ref,
                gq_ref, wuq_ref, gkv_ref, wukv_ref,
                x1_ref, u_ref, q_ref, k_ref, v_ref):
  rows_per = x_ref.shape[0] // ROW_SPLIT
  half = QK_ROPE // 2
  lane = lax.broadcasted_iota(jnp.int32, (rows_per, LANES), 1)
  ones_col = (lane == V_HEAD).astype(F32)
  for part in range(ROW_SPLIT):
    rows = slice(part * rows_per, (part + 1) * rows_per)
    x = x_ref[rows, :]
    h = _rms(x, g1pre_ref[...]).astype(BF16)
    f = _swiglu(h, wgu_ref, wd_ref)
    x1 = x + 0.5 * _rms(f, g1post_ref[...])
    x1_ref[rows, :] = x1

    h2 = _rms(x1, gmix_ref[...]).astype(BF16)
    z = jnp.dot(h2, win_ref[...], preferred_element_type=F32)
    u_ref[rows, :] = z[:, :D_SSM]
    q_c = z[:, D_SSM:D_SSM + Q_RANK]
    kv_c = z[:, D_SSM + Q_RANK:D_SSM + Q_RANK + KV_RANK]
    kr = z[:, D_SSM + Q_RANK + KV_RANK:]

    kr_sw = jnp.where(lane < half, pltpu.roll(kr, LANES - half, 1), pltpu.roll(kr, half, 1))
    k_pe = pltpu.roll(kr * ck_ref[rows, :] + kr_sw * sk_ref[rows, :], QK_NOPE, 1)

    qn = _rms(q_c, gq_ref[...]).astype(BF16)
    q_raw = jnp.dot(qn, wuq_ref[...], preferred_element_type=F32)
    kvn = _rms(kv_c, gkv_ref[...]).astype(BF16)
    kv_raw = jnp.dot(kvn, wukv_ref[...], preferred_element_type=F32)

    cq = cq_ref[rows, :]
    sq = sq_ref[rows, :]
    for hd in range(N_HEADS):
      lo = hd * HEAD_PAD
      t = q_raw[:, lo:lo + HEAD_PAD]
      t_sw = jnp.where(lane < QK_NOPE + half, pltpu.roll(t, LANES - half, 1), pltpu.roll(t, half, 1))
      q_ref[rows, lo:lo + HEAD_PAD] = (t * cq + t_sw * sq).astype(BF16)
      k_ref[rows, lo:lo + HEAD_PAD] = (kv_raw[:, lo:lo + HEAD_PAD] + k_pe).astype(BF16)
      v_ref[rows, lo:lo + HEAD_PAD] = (
          kv_raw[:, D_HEADS_PAD + lo:D_HEADS_PAD + lo + HEAD_PAD] + ones_col).astype(BF16)


def _pre_call(x2d, seq_len, tabs, w):
  n = x2d.shape[0]
  nlt = seq_len // TM
  row = lambda i: (i, 0)
  pos = lambda i: (i % nlt, 0)
  tab_spec = pl.BlockSpec((TM, LANES), pos)
  in_specs = [
      pl.BlockSpec((TM, D_MODEL), row), tab_spec, tab_spec, tab_spec, tab_spec,
      _const_spec((1, D_MODEL)), _const_spec((D_MODEL, 2 * D_FF)), _const_spec((D_FF, D_MODEL)),
      _const_spec((1, D_MODEL)), _const_spec((1, D_MODEL)), _const_spec((D_MODEL, D_IN_PAD)),
      _const_spec((1, Q_RANK)), _const_spec((Q_RANK, D_HEADS_PAD)),
      _const_spec((1, KV_RANK)), _const_spec((KV_RANK, 2 * D_HEADS_PAD)),
  ]
  out_shape = [
      jax.ShapeDtypeStruct((n, D_MODEL), F32), jax.ShapeDtypeStruct((n, D_SSM), F32),
      jax.ShapeDtypeStruct((n, D_HEADS_PAD), BF16), jax.ShapeDtypeStruct((n, D_HEADS_PAD), BF16),
      jax.ShapeDtypeStruct((n, D_HEADS_PAD), BF16),
  ]
  out_specs = [
      pl.BlockSpec((TM, D_MODEL), row), pl.BlockSpec((TM, D_SSM), row),
      pl.BlockSpec((TM, D_HEADS_PAD), row), pl.BlockSpec((TM, D_HEADS_PAD), row),
      pl.BlockSpec((TM, D_HEADS_PAD), row),
  ]
  return pl.pallas_call(
      _pre_kernel, grid=(n // TM,), in_specs=in_specs, out_specs=out_specs, out_shape=out_shape,
      compiler_params=pltpu.CompilerParams(
          dimension_semantics=("arbitrary",), vmem_limit_bytes=VMEM_LIMIT_BYTES),
      name="pre",
  )(x2d, tabs["cq"], tabs["sq"], tabs["ck"], tabs["sk"],
    w["g_ffn1_pre"], w["wgu1"], w["wd1"], w["g_ffn1_post"], w["g_mix_pre"], w["w_in"],
    w["g_q"], w["w_uq"], w["g_kv"], w["w_ukv"])


def _attn_kernel(q_ref, k_ref, v_ref, g_ref, o_ref, s_ref):
  tq = q_ref.shape[0]
  lane = lax.broadcasted_iota(jnp.int32, (tq, HEAD_PAD), 1)

  def scores(hd):
    lo = hd * HEAD_PAD
    s = lax.dot_general(q_ref[:, lo:lo + HEAD_PAD], k_ref[:, lo:lo + HEAD_PAD],
                        (((1,), (1,)), ((), ())), preferred_element_type=F32)
    s_ref[hd % 2] = s
    return jnp.max(s, axis=-1, keepdims=True)

  heads = []
  ssq = jnp.zeros((tq, 1), F32)
  m_next = scores(0)
  for hd in range(N_HEADS):
    lo = hd * HEAD_PAD
    m = m_next
    if hd + 1 < N_HEADS:
      m_next = scores(hd + 1)
    p = jnp.exp2(s_ref[hd % 2] - m).astype(BF16)
    oa = jnp.dot(p, v_ref[:, lo:lo + HEAD_PAD], preferred_element_type=F32)
    o = jnp.where(lane < V_HEAD, oa / oa[:, V_HEAD:V_HEAD + 1], 0.0)
    ssq = ssq + jnp.sum(o * o, axis=-1, keepdims=True)
    heads.append(o)
  inv = lax.rsqrt(ssq * (1.0 / D_ATTN) + EPS)
  for j in range(N_HEADS // 2):
    pair = heads[2 * j] + pltpu.roll(heads[2 * j + 1], V_HEAD, 1)
    o_ref[:, j * LANES:(j + 1) * LANES] = (
        pair * inv * g_ref[:, j * LANES:(j + 1) * LANES]).astype(BF16)


def _attn_call(q, k, v, g_att, batch, seq_len):
  n = q.shape[0]
  nqt = seq_len // TQ
  kv_spec = pl.BlockSpec((seq_len, D_HEADS_PAD), lambda b, i: (b, 0), pipeline_mode=pl.Buffered(1))
  return pl.pallas_call(
      _attn_kernel, grid=(batch, nqt),
      in_specs=[pl.BlockSpec((TQ, D_HEADS_PAD), lambda b, i: (b * nqt + i, 0)), kv_spec, kv_spec,
                _const_spec((1, D_ATTN))],
      out_specs=pl.BlockSpec((TQ, D_ATTN), lambda b, i: (b * nqt + i, 0)),
      out_shape=jax.ShapeDtypeStruct((n, D_ATTN), BF16),
      scratch_shapes=[pltpu.VMEM((2, TQ, seq_len), F32)],
      compiler_params=pltpu.CompilerParams(
          dimension_semantics=("arbitrary", "arbitrary"), vmem_limit_bytes=VMEM_LIMIT_BYTES),
      name="attn",
  )(q, k, v, g_att)


def _powers(n, xr, xi):
  mag = jnp.exp(n * xr)
  return mag * jnp.cos(n * xi), mag * jnp.sin(n * xi)


def _s5_gen_kernel(lrow_ref, bt_ref, cc_ref, tz_ref, bx_ref, cs_ref, a_ref):
  ch, c, p2 = CHUNK, SSM_GROUP, 2 * SSM_STATE
  is_f = lax.broadcasted_iota(jnp.int32, (ch * c, p2), 1) < SSM_STATE
  n_tab = lax.broadcasted_iota(jnp.int32, (2 * ch, p2), 0).astype(F32)

  def each_row_repeated(tab, idx):
    return jnp.concatenate([jnp.broadcast_to(tab[i:i + 1, :], (c, p2)) for i in idx], axis=0)

  def by_direction(tab, idx_f, idx_b):
    return jnp.where(is_f, each_row_repeated(tab, idx_f), each_row_repeated(tab, idx_b))

  def tiled(tab):
    return jnp.concatenate([tab] * ch, axis=0)

  lre, lim = lrow_ref[0:1, :], lrow_ref[1:2, :]
  dt = jnp.exp(lrow_ref[2:3, :])
  pw_r, pw_i = _powers(n_tab, lre * dt, lim * dt)
  ar, ai = pw_r[1:2, :], pw_i[1:2, :]
  den = lre * lre + lim * lim
  cr = ((ar - 1.0) * lre + ai * lim) / den
  ci = (ai * lre - (ar - 1.0) * lim) / den
  br, bi = tiled(cr * bt_ref[0] - ci * bt_ref[1]), tiled(cr * bt_ref[1] + ci * bt_ref[0])

  up, down = list(range(ch)), [ch - 1 - j for j in range(ch)]
  pr, pi = by_direction(pw_r, down, up), by_direction(pw_i, down, up)
  bx_ref[:, :p2] = (pr * br - pi * bi).astype(BF16)
  bx_ref[:, p2:] = (pr * bi + pi * br).astype(BF16)

  f_idx, b_idx = [i + 1 for i in range(ch)], [ch - i for i in range(ch)]
  pr, pi = by_direction(pw_r, f_idx, b_idx), by_direction(pw_i, f_idx, b_idx)
  tr, ti = tiled(cc_ref[0]), tiled(cc_ref[1])
  cst = jnp.concatenate([tr * pr - ti * pi, -(tr * pi + ti * pr)], axis=1)
  cs_ref[...] = cst.T.astype(BF16)

  pr, pi = each_row_repeated(pw_r, up), each_row_repeated(pw_i, up)
  wr, wi = pr * br - pi * bi, pr * bi + pi * br
  spread = (lax.broadcasted_iota(jnp.int32, (c, ch * c), 1) & (c - 1)
            == lax.broadcasted_iota(jnp.int32, (c, ch * c), 0)).astype(F32)
  nt = (((1,), (1,)), ((), ()))
  split = lambda x: (x.astype(BF16), (x - x.astype(BF16).astype(F32)).astype(BF16))
  spread_b = spread.astype(BF16)
  (cr_h, cr_l), (ci_h, ci_l) = split(cc_ref[0]), split(cc_ref[1])

  def dot3(x, y_h, y_l):
    x_h, x_l = split(x)
    d = lambda a, b: lax.dot_general(a, b, nt, preferred_element_type=F32)
    return d(x_h, y_h) + d(x_h, y_l) + d(x_l, y_h)

  def lag(mask):
    m = dot3(jnp.where(mask, wr, 0.0), cr_h, cr_l) - dot3(jnp.where(mask, wi, 0.0), ci_h, ci_l)
    m_h, m_l = split(m)
    return (jnp.dot(m_h, spread_b, preferred_element_type=F32)
            + jnp.dot(m_l, spread_b, preferred_element_type=F32))

  lag_f, lag_b = lag(is_f), lag(jnp.logical_not(is_f))
  lane_i = lax.broadcasted_iota(jnp.int32, (c, ch * c), 1) >> 4
  for j in range(ch):
    blk = jnp.where(lane_i == j, lag_f[:c, :] + lag_b[:c, :], 0.0)
    for e in range(1, ch - j):
      blk = jnp.where(lane_i == j + e, lag_f[e * c:(e + 1) * c, :], blk)
    for e in range(1, j + 1):
      blk = jnp.where(lane_i == j - e, lag_b[e * c:(e + 1) * c, :], blk)
    tz_ref[j * c:(j + 1) * c, :] = blk.astype(BF16)

  a_ref[0] = jnp.concatenate([pw_r[ch:ch + 1, :]] * 2, axis=1)
  a_ref[1] = jnp.concatenate([pw_i[ch:ch + 1, :]] * 2, axis=1)


def _s5_operators(fwd, bwd):
  g, p, c = N_SSM_GROUPS, SSM_STATE, SSM_GROUP
  both = lambda x, y: jnp.concatenate([x, y], axis=-1)
  ldt = both(jnp.broadcast_to(fwd[2][:, None], (g, p)), jnp.broadcast_to(bwd[2][:, None], (g, p)))
  lrow = jnp.stack([both(fwd[0], bwd[0]), both(fwd[1], bwd[1]), ldt], axis=1)
  bt = jnp.stack([both(jnp.swapaxes(fwd[i], 1, 2), jnp.swapaxes(bwd[i], 1, 2)) for i in (3, 4)], axis=1)
  cc = jnp.stack([both(fwd[i], bwd[i]) for i in (5, 6)], axis=1)
  spec = lambda *s: pl.BlockSpec((None,) + s, lambda i: (i,) + (0,) * len(s))
  tz, bx, cs, a = pl.pallas_call(
      _s5_gen_kernel, grid=(g,),
      in_specs=[spec(3, 2 * p), spec(2, c, 2 * p), spec(2, c, 2 * p)],
      out_specs=[spec(G_LANES, G_LANES)] * 3 + [spec(2, 1, G_LANES)],
      out_shape=[jax.ShapeDtypeStruct((g, G_LANES, G_LANES), BF16)] * 3
      + [jax.ShapeDtypeStruct((g, 2, 1, G_LANES), F32)],
      compiler_params=pltpu.CompilerParams(dimension_semantics=("arbitrary",)),
      name="s5_operators",
  )(lrow, bt, cc)
  trans = jnp.broadcast_to(jnp.swapaxes(a, 0, 1).reshape(2, 1, g * G_LANES), (2, SUBLANES, g * G_LANES))
  return tz, bx, cs, trans


def _step_permutation():
  src = jnp.arange(HALF * LANES)
  j, gl, ci = src // LANES, (src // SSM_GROUP) % GROUPS_PER_TILE, src % SSM_GROUP
  dst = gl * LANES + j * SSM_GROUP + ci
  return (dst[:, None] == jnp.arange(HALF * LANES)[None, :]).astype(BF16)


def _s5_in_kernel(u_ref, perm_ref, bx_ref, up_ref, x_ref, ut_ref):
  nb, tb = SUBLANES, u_ref.shape[1]
  nc = tb // CHUNK
  r = nc * nb
  for b in range(nb):
    ut_ref[pl.ds(b, tb, stride=nb), :] = u_ref[b]
  halves = []
  for h in range(2):
    steps = []
    for j in range(h * HALF, (h + 1) * HALF):
      steps.append(jnp.concatenate(
          [ut_ref[(ci * CHUNK + j) * nb:(ci * CHUNK + j + 1) * nb, :] for ci in range(nc)], axis=0))
    halves.append(jnp.concatenate(steps, axis=1))
  ucat = jnp.concatenate(halves, axis=0).astype(BF16)
  uperm = jnp.dot(ucat, perm_ref[...], preferred_element_type=F32).astype(BF16)
  for gl in range(GROUPS_PER_TILE):
    ug = jnp.concatenate([uperm[h * r:(h + 1) * r, gl * LANES:(gl + 1) * LANES] for h in range(2)], axis=1)
    up_ref[:, gl * G_LANES:(gl + 1) * G_LANES] = ug
    x_ref[:, gl * G_LANES:(gl + 1) * G_LANES] = jnp.dot(ug, bx_ref[gl], preferred_element_type=F32)


def _s5_scan_kernel(x_ref, a_ref, sr_ref, sf_ref, sb_ref):
  nb = SUBLANES
  nr, w = x_ref.shape
  nc = nr // nb
  lane = lax.broadcasted_iota(jnp.int32, (nb, w), 1)
  is_fwd = (lane & (LANES - 1)) < SSM_STATE
  a_re, a_im = a_ref[0], a_ref[1]

  def body(k, s):
    rf = pl.multiple_of(k * nb, nb)
    rb = pl.multiple_of((nc - 1 - k) * nb, nb)
    sf_ref[pl.ds(rf, nb), :] = s
    sb_ref[pl.ds(rb, nb), :] = s
    x = jnp.where(is_fwd, x_ref[pl.ds(rf, nb), :], x_ref[pl.ds(rb, nb), :])
    parts = []
    for g in range(w // G_LANES):
      re = slice(g * G_LANES, g * G_LANES + LANES)
      im = slice(g * G_LANES + LANES, (g + 1) * G_LANES)
      parts.append(a_re[:, re] * s[:, re] - a_im[:, re] * s[:, im] + x[:, re])
      parts.append(a_re[:, im] * s[:, im] + a_im[:, im] * s[:, re] + x[:, im])
    return jnp.concatenate(parts, axis=1)

  lax.fori_loop(0, nc, body, jnp.zeros((nb, w), F32), unroll=4)
  row_is_fwd = (lax.broadcasted_iota(jnp.int32, (1, w), 1) & (LANES - 1)) < SSM_STATE
  sr_ref[...] = jnp.where(row_is_fwd, sf_ref[...], sb_ref[...]).astype(BF16)


def _s5_out_kernel(up_ref, sr_ref, tz_ref, cs_ref, perm_ref, y_ref, yt_ref):
  nb, tb = SUBLANES, y_ref.shape[1]
  nc = tb // CHUNK
  r = nc * nb
  ys = []
  for gl in range(GROUPS_PER_TILE):
    sl = slice(gl * G_LANES, (gl + 1) * G_LANES)
    ys.append(jnp.dot(up_ref[:, sl], tz_ref[gl], preferred_element_type=F32)
              + jnp.dot(sr_ref[:, sl], cs_ref[gl], preferred_element_type=F32))
  ycat = jnp.concatenate(
      [jnp.concatenate([y[:, h * LANES:(h + 1) * LANES] for y in ys], axis=1) for h in range(2)], axis=0)
  hi = ycat.astype(BF16)
  lo = (ycat - hi.astype(F32)).astype(BF16)
  ynat = (jnp.dot(hi, perm_ref[...], preferred_element_type=F32)
          + jnp.dot(lo, perm_ref[...], preferred_element_type=F32))
  for h in range(2):
    for j in range(HALF):
      blk = ynat[h * r:(h + 1) * r, j * LANES:(j + 1) * LANES]
      for ci in range(nc):
        t = ci * CHUNK + h * HALF + j
        yt_ref[t * nb:(t + 1) * nb, :] = blk[ci * nb:(ci + 1) * nb, :]
  for b in range(nb):
    y_ref[b] = yt_ref[pl.ds(b, tb, stride=nb), :]


def _s5_mix(u3, tz, bx, cs, trans, perm):
  nb, seq_len, _ = u3.shape
  assert nb == SUBLANES and seq_len % TB == 0
  nblk = seq_len // TB
  r = TB // CHUNK * nb
  nr = seq_len // CHUNK * nb
  width = N_SSM_GROUPS * G_LANES
  qw = GROUPS_PER_TILE * G_LANES
  seq_spec = pl.BlockSpec((nb, TB, LANES), lambda q, i: (0, i, q))
  row_spec = pl.BlockSpec((r, qw), lambda q, i: (i, q))
  op_spec = pl.BlockSpec((GROUPS_PER_TILE, G_LANES, G_LANES), lambda q, i: (q, 0, 0))
  perm_spec = _const_spec((HALF * LANES, HALF * LANES))
  params = pltpu.CompilerParams(
      dimension_semantics=("arbitrary", "arbitrary"), vmem_limit_bytes=VMEM_LIMIT_BYTES)
  up, x = pl.pallas_call(
      _s5_in_kernel, grid=(N_QUARTERS, nblk),
      in_specs=[seq_spec, perm_spec, op_spec], out_specs=[row_spec, row_spec],
      out_shape=[jax.ShapeDtypeStruct((nr, width), BF16), jax.ShapeDtypeStruct((nr, width), F32)],
      scratch_shapes=[pltpu.VMEM((TB * nb, LANES), F32)],
      compiler_params=params, name="s5_in",
  )(u3, perm, bx)
  sr = pl.pallas_call(
      _s5_scan_kernel, grid=(width // SCAN_W,),
      in_specs=[pl.BlockSpec((nr, SCAN_W), lambda j: (0, j)),
                pl.BlockSpec((2, nb, SCAN_W), lambda j: (0, 0, j))],
      out_specs=pl.BlockSpec((nr, SCAN_W), lambda j: (0, j)),
      out_shape=jax.ShapeDtypeStruct((nr, width), BF16),
      scratch_shapes=[pltpu.VMEM((nr, SCAN_W), F32), pltpu.VMEM((nr, SCAN_W), F32)],
      compiler_params=pltpu.CompilerParams(
          dimension_semantics=("arbitrary",), vmem_limit_bytes=VMEM_LIMIT_BYTES),
      name="s5_scan",
  )(x, trans)
  return pl.pallas_call(
      _s5_out_kernel, grid=(N_QUARTERS, nblk),
      in_specs=[row_spec, row_spec, op_spec, op_spec, perm_spec], out_specs=seq_spec,
      out_shape=jax.ShapeDtypeStruct(u3.shape, F32),
      scratch_shapes=[pltpu.VMEM((TB * nb, LANES), F32)],
      compiler_params=params, name="s5_out",
  )(up, sr, tz, cs, perm.T)


def _post_kernel(x1_ref, u_ref, yc_ref, ya_ref,
                 dskip_ref, wglu_ref, bglu_ref, gssm_ref, wos_ref, woa_ref, gpost_ref,
                 g2pre_ref, wgu_ref, wd_ref, g2post_ref, o_ref):
  rows_per = x1_ref.shape[0] // ROW_SPLIT
  for part in range(ROW_SPLIT):
    rows = slice(part * rows_per, (part + 1) * rows_per)
    y = yc_ref[rows, :] + dskip_ref[...] * u_ref[rows, :]
    y = y * (0.5 * (1.0 + jnp.tanh(math.sqrt(2.0 / math.pi) * (y + 0.044715 * (y * y * y)))))
    gate = jnp.dot(y.astype(BF16), wglu_ref[...], preferred_element_type=F32) + bglu_ref[...]
    y = y * jax.nn.sigmoid(gate)
    y_ssm = _rms(y, gssm_ref[...]).astype(BF16)
    m = (jnp.dot(y_ssm, wos_ref[...], preferred_element_type=F32)
         + jnp.dot(ya_ref[rows, :], woa_ref[...], preferred_element_type=F32))
    x2 = x1_ref[rows, :] + _rms(m, gpost_ref[...])
    h = _rms(x2, g2pre_ref[...]).astype(BF16)
    f = _swiglu(h, wgu_ref, wd_ref)
    o_ref[rows, :] = x2 + 0.5 * _rms(f, g2post_ref[...])


def _post_call(x1, u, yc, ya, w):
  n = x1.shape[0]
  row = lambda i: (i, 0)
  in_specs = [
      pl.BlockSpec((TM, D_MODEL), row), pl.BlockSpec((TM, D_SSM), row),
      pl.BlockSpec((TM, D_SSM), row), pl.BlockSpec((TM, D_ATTN), row),
      _const_spec((1, D_SSM)), _const_spec((D_SSM, D_SSM)), _const_spec((1, D_SSM)),
      _const_spec((1, D_SSM)), _const_spec((D_SSM, D_MODEL)), _const_spec((D_ATTN, D_MODEL)),
      _const_spec((1, D_MODEL)), _const_spec((1, D_MODEL)),
      _const_spec((D_MODEL, 2 * D_FF)), _const_spec((D_FF, D_MODEL)), _const_spec((1, D_MODEL)),
  ]
  return pl.pallas_call(
      _post_kernel, grid=(n // TM,), in_specs=in_specs,
      out_specs=pl.BlockSpec((TM, D_MODEL), row),
      out_shape=jax.ShapeDtypeStruct((n, D_MODEL), F32),
      compiler_params=pltpu.CompilerParams(
          dimension_semantics=("arbitrary",), vmem_limit_bytes=VMEM_LIMIT_BYTES),
      name="post",
  )(x1, u, yc, ya,
    w["d_skip"], w["w_glu"], w["b_glu"], w["g_ssm_out"], w["w_out_ssm"], w["w_out_att"],
    w["g_mix_post"], w["g_ffn2_pre"], w["wgu2"], w["wd2"], w["g_ffn2_post"])


def _rope_tables(length):
  inv = 1.0 / (ROPE_THETA ** (jnp.arange(0, QK_ROPE, 2, dtype=F32) / QK_ROPE))
  ang = jnp.arange(length, dtype=F32)[:, None] * inv[None, :]
  cos, sin = jnp.cos(ang), jnp.sin(ang)
  zk = jnp.zeros((length, LANES - QK_ROPE), F32)
  ck = jnp.concatenate([cos, cos, zk], axis=1)
  sk = jnp.concatenate([-sin, sin, zk], axis=1)
  qs = (QK_NOPE + QK_ROPE) ** -0.5 * LOG2E
  zq = jnp.zeros((length, LANES - QK_NOPE - QK_ROPE), F32)
  cq = qs * jnp.concatenate([jnp.ones((length, QK_NOPE), F32), cos, cos, zq], axis=1)
  sq = qs * jnp.concatenate([jnp.zeros((length, QK_NOPE), F32), -sin, sin, zq], axis=1)
  return {"cq": cq, "sq": sq, "ck": ck, "sk": sk}


def _head_tiles(wmat, width):
  k = wmat.shape[0]
  wh = wmat.reshape(k, N_HEADS, width)
  return jnp.pad(wh, ((0, 0), (0, 0), (0, HEAD_PAD - width))).reshape(k, D_HEADS_PAD)


def kernel(x_prompt, x_sample, g_ffn1_pre, w_ffn1_gate, w_ffn1_up, w_ffn1_down, g_ffn1_post, g_mix_pre, w_in, lam_re_fwd, lam_im_fwd, log_dt_fwd, b_re_fwd, b_im_fwd, c_re_fwd, c_im_fwd, lam_re_bwd, lam_im_bwd, log_dt_bwd, b_re_bwd, b_im_bwd, c_re_bwd, c_im_bwd, d_skip, w_glu, b_glu, g_ssm_out, g_q, w_uq, g_kv, w_ukv, g_att_out, w_out, g_mix_post, g_ffn2_pre, w_ffn2_gate, w_ffn2_up, w_ffn2_down, g_ffn2_post):
  depth = w_in.shape[0]
  tables = {}
  perm = _step_permutation()
  s5_ops = [
      _s5_operators(
          (lam_re_fwd[l], lam_im_fwd[l], log_dt_fwd[l], b_re_fwd[l], b_im_fwd[l], c_re_fwd[l], c_im_fwd[l]),
          (lam_re_bwd[l], lam_im_bwd[l], log_dt_bwd[l], b_re_bwd[l], b_im_bwd[l], c_re_bwd[l], c_im_bwd[l]))
      for l in range(depth)]

  def trunk(x):
    batch, seq_len, _ = x.shape
    assert batch == SUBLANES and seq_len % TM == 0 and seq_len % TQ == 0 and seq_len % TB == 0
    if seq_len not in tables:
      tables[seq_len] = _rope_tables(seq_len)
    h = x.reshape(batch * seq_len, D_MODEL)
    for l in range(depth):
      row = lambda v: v[l].reshape(1, -1)
      win = w_in[l]
      win_pad = jnp.concatenate(
          [win, jnp.zeros((D_MODEL, D_IN_PAD - win.shape[1]), F32)], axis=1).astype(BF16)
      wukv = w_ukv[l].reshape(KV_RANK, N_HEADS, QK_NOPE + V_HEAD)
      w = {
          "g_ffn1_pre": row(g_ffn1_pre),
          "wgu1": jnp.concatenate([w_ffn1_gate[l], w_ffn1_up[l]], axis=1).astype(BF16),
          "wd1": w_ffn1_down[l].astype(BF16), "g_ffn1_post": row(g_ffn1_post),
          "g_mix_pre": row(g_mix_pre), "w_in": win_pad,
          "g_q": row(g_q), "w_uq": _head_tiles(w_uq[l], QK_NOPE + QK_ROPE).astype(BF16),
          "g_kv": row(g_kv),
          "w_ukv": jnp.concatenate(
              [_head_tiles(wukv[:, :, :QK_NOPE].reshape(KV_RANK, -1), QK_NOPE),
               _head_tiles(wukv[:, :, QK_NOPE:].reshape(KV_RANK, -1), V_HEAD)], axis=1).astype(BF16),
          "d_skip": row(d_skip), "w_glu": w_glu[l].astype(BF16), "b_glu": row(b_glu),
          "g_ssm_out": row(g_ssm_out),
          "w_out_ssm": w_out[l][:D_SSM].astype(BF16), "w_out_att": w_out[l][D_SSM:].astype(BF16),
          "g_mix_post": row(g_mix_post), "g_ffn2_pre": row(g_ffn2_pre),
          "wgu2": jnp.concatenate([w_ffn2_gate[l], w_ffn2_up[l]], axis=1).astype(BF16),
          "wd2": w_ffn2_down[l].astype(BF16), "g_ffn2_post": row(g_ffn2_post),
      }
      x1, u, q, k, v = _pre_call(h, seq_len, tables[seq_len], w)
      ya = _attn_call(q, k, v, row(g_att_out), batch, seq_len)
      u3 = u.reshape(batch, seq_len, D_SSM)
      yc = _s5_mix(u3, *s5_ops[l], perm).reshape(u.shape)
      h = _post_call(x1, u, yc, ya, w)
    return h.reshape(batch, seq_len, D_MODEL)

  return (trunk(x_prompt), trunk(x_sample))
```

```python
import functools
import math

import jax
import jax.numpy as jnp
import numpy as np
from jax import lax
from jax.experimental import pallas as pl
from jax.experimental.pallas import tpu as pltpu

F32 = jnp.float32
BF16 = jnp.bfloat16

D_MODEL = 1024
D_FF = 2816
D_SSM = 512
SSM_GROUP = 16
N_SSM_GROUPS = 32
SSM_STATE = 64
N_HEADS = 8
QK_NOPE = 64
QK_ROPE = 32
V_HEAD = 64
D_ATTN = N_HEADS * V_HEAD
Q_RANK = 384
KV_RANK = 256
ROPE_THETA = 10000.0
EPS = 1e-6
LOG2E = math.log2(math.e)

LANES = 128
SUBLANES = 8
VMEM_LIMIT_BYTES = 56 * 1024 * 1024

HEAD_PAD = LANES
D_HEADS_PAD = N_HEADS * HEAD_PAD
D_IN_PAD = D_SSM + Q_RANK + KV_RANK + LANES
MXU_DIM = 256
FF_CHUNKS = ((0, 6 * MXU_DIM), (6 * MXU_DIM, D_FF))
ROW_SPLIT = 2

TM = 512
SCORE_BYTES = 16 * 1024 * 1024
ATTN_VMEM_LIMIT_BYTES = 60 * 1024 * 1024
GROUPS_PER_TILE = LANES // SSM_GROUP
N_QUARTERS = D_SSM // LANES
CHUNK = 16
HALF = CHUNK // 2
G_LANES = CHUNK * SSM_GROUP
TB = 512
SCAN_W = 512


def _const_spec(shape):
  zeros = (0,) * len(shape)
  return pl.BlockSpec(shape, lambda *_: zeros, pipeline_mode=pl.Buffered(1))


def _rms(x, g):
  ms = jnp.mean(x * x, axis=-1, keepdims=True)
  return x * lax.rsqrt(ms + EPS) * g


def _swiglu(h, wgu_ref, wd_ref):
  acc = None
  for lo, hi in FF_CHUNKS:
    gate = jnp.dot(h, wgu_ref[:, lo:hi], preferred_element_type=F32)
    up = jnp.dot(h, wgu_ref[:, D_FF + lo:D_FF + hi], preferred_element_type=F32)
    act = (gate * jax.nn.sigmoid(gate) * up).astype(BF16)
    part = jnp.dot(act, wd_ref[lo:hi, :], preferred_element_type=F32)
    acc = part if acc is None else acc + part
  return acc


def _pre_kernel(x_ref, cq_ref, sq_ref, ck_ref, sk_ref,
                g1pre_ref, wgu_ref, wd_ref, g1post_ref, gmix_ref, win_ref,
                gq_ref, wuq_ref, gkv_ref, wukv_ref,
                x1_ref, u_ref, q_ref, k_ref, v_ref):
  rows_per = x_ref.shape[0] // ROW_SPLIT
  half = QK_ROPE // 2
  lane = lax.broadcasted_iota(jnp.int32, (rows_per, LANES), 1)
  ones_col = (lane == V_HEAD).astype(F32)
  for part in range(ROW_SPLIT):
    rows = slice(part * rows_per, (part + 1) * rows_per)
    x = x_ref[rows, :]
    h = _rms(x, g1pre_ref[...]).astype(BF16)
    f = _swiglu(h, wgu_ref, wd_ref)
    x1 = x + 0.5 * _rms(f, g1post_ref[...])
    x1_ref[rows, :] = x1

    h2 = _rms(x1, gmix_ref[...]).astype(BF16)
    z = jnp.dot(h2, win_ref[...], preferred_element_type=F32)
    u_ref[rows, :] = z[:, :D_SSM]
    q_c = z[:, D_SSM:D_SSM + Q_RANK]
    kv_c = z[:, D_SSM + Q_RANK:D_SSM + Q_RANK + KV_RANK]
    kr = z[:, D_SSM + Q_RANK + KV_RANK:]

    kr_sw = jnp.where(lane < half, pltpu.roll(kr, LANES - half, 1), pltpu.roll(kr, half, 1))
    k_pe = pltpu.roll(kr * ck_ref[rows, :] + kr_sw * sk_ref[rows, :], QK_NOPE, 1)

    qn = _rms(q_c, gq_ref[...]).astype(BF16)
    q_raw = jnp.dot(qn, wuq_ref[...], preferred_element_type=F32)
    kvn = _rms(kv_c, gkv_ref[...]).astype(BF16)
    kv_raw = jnp.dot(kvn, wukv_ref[...], preferred_element_type=F32)

    cq = cq_ref[rows, :]
    sq = sq_ref[rows, :]
    for hd in range(N_HEADS):
      lo = hd * HEAD_PAD
      t = q_raw[:, lo:lo + HEAD_PAD]
      t_sw = jnp.where(lane < QK_NOPE + half, pltpu.roll(t, LANES - half, 1), pltpu.roll(t, half, 1))
      q_ref[rows, lo:lo + HEAD_PAD] = (t * cq + t_sw * sq).astype(BF16)
      k_ref[rows, lo:lo + HEAD_PAD] = (kv_raw[:, lo:lo + HEAD_PAD] + k_pe).astype(BF16)
      v_ref[rows, lo:lo + HEAD_PAD] = (
          kv_raw[:, D_HEADS_PAD + lo:D_HEADS_PAD + lo + HEAD_PAD] + ones_col).astype(BF16)


def _pre_call(x2d, seq_len, tabs, w):
  n = x2d.shape[0]
  nlt = seq_len // TM
  row = lambda i: (i, 0)
  pos = lambda i: (i % nlt, 0)
  tab_spec = pl.BlockSpec((TM, LANES), pos)
  in_specs = [
      pl.BlockSpec((TM, D_MODEL), row), tab_spec, tab_spec, tab_spec, tab_spec,
      _const_spec((1, D_MODEL)), _const_spec((D_MODEL, 2 * D_FF)), _const_spec((D_FF, D_MODEL)),
      _const_spec((1, D_MODEL)), _const_spec((1, D_MODEL)), _const_spec((D_MODEL, D_IN_PAD)),
      _const_spec((1, Q_RANK)), _const_spec((Q_RANK, D_HEADS_PAD)),
      _const_spec((1, KV_RANK)), _const_spec((KV_RANK, 2 * D_HEADS_PAD)),
  ]
  out_shape = [
      jax.ShapeDtypeStruct((n, D_MODEL), F32), jax.ShapeDtypeStruct((n, D_SSM), F32),
      jax.ShapeDtypeStruct((n, D_HEADS_PAD), BF16), jax.ShapeDtypeStruct((n, D_HEADS_PAD), BF16),
      jax.ShapeDtypeStruct((n, D_HEADS_PAD), BF16),
  ]
  out_specs = [
      pl.BlockSpec((TM, D_MODEL), row), pl.BlockSpec((TM, D_SSM), row),
      pl.BlockSpec((TM, D_HEADS_PAD), row), pl.BlockSpec((TM, D_HEADS_PAD), row),
      pl.BlockSpec((TM, D_HEADS_PAD), row),
  ]
  return pl.pallas_call(
      _pre_kernel, grid=(n // TM,), in_specs=in_specs, out_specs=out_specs, out_shape=out_shape,
      compiler_params=pltpu.CompilerParams(
          dimension_semantics=("arbitrary",), vmem_limit_bytes=VMEM_LIMIT_BYTES),
      name="pre",
  )(x2d, tabs["cq"], tabs["sq"], tabs["ck"], tabs["sk"],
    w["g_ffn1_pre"], w["wgu1"], w["wd1"], w["g_ffn1_post"], w["g_mix_pre"], w["w_in"],
    w["g_q"], w["w_uq"], w["g_kv"], w["w_ukv"])


def _attn_kernel(q_ref, k_ref, v_ref, g_ref, o_ref, s_ref):
  tq = q_ref.shape[0]
  lane = lax.broadcasted_iota(jnp.int32, (tq, HEAD_PAD), 1)

  def scores(hd):
    lo = hd * HEAD_PAD
    s = lax.dot_general(q_ref[:, lo:lo + HEAD_PAD], k_ref[:, lo:lo + HEAD_PAD],
                        (((1,), (1,)), ((), ())), preferred_element_type=F32)
    s_ref[hd % 2] = s
    return jnp.max(s, axis=-1, keepdims=True)

  heads = []
  ssq = jnp.zeros((tq, 1), F32)
  m_next = scores(0)
  for hd in range(N_HEADS):
    lo = hd * HEAD_PAD
    m = m_next
    if hd + 1 < N_HEADS:
      m_next = scores(hd + 1)
    p = jnp.exp2(s_ref[hd % 2] - m).astype(BF16)
    oa = jnp.dot(p, v_ref[:, lo:lo + HEAD_PAD], preferred_element_type=F32)
    o = jnp.where(lane < V_HEAD, oa / oa[:, V_HEAD:V_HEAD + 1], 0.0)
    ssq = ssq + jnp.sum(o * o, axis=-1, keepdims=True)
    heads.append(o)
  inv = lax.rsqrt(ssq * (1.0 / D_ATTN) + EPS)
  for j in range(N_HEADS // 2):
    pair = heads[2 * j] + pltpu.roll(heads[2 * j + 1], V_HEAD, 1)
    o_ref[:, j * LANES:(j + 1) * LANES] = (
        pair * inv * g_ref[:, j * LANES:(j + 1) * LANES]).astype(BF16)


def _attn_call(q, k, v, g_att, batch, seq_len):
  n = q.shape[0]
  tq = min(seq_len, SCORE_BYTES // (2 * seq_len * 4))
  assert seq_len % tq == 0
  nqt = seq_len // tq
  kv_spec = pl.BlockSpec((seq_len, D_HEADS_PAD), lambda b, i: (b, 0))
  return pl.pallas_call(
      _attn_kernel, grid=(batch, nqt),
      in_specs=[pl.BlockSpec((tq, D_HEADS_PAD), lambda b, i: (b * nqt + i, 0)), kv_spec, kv_spec,
                _const_spec((1, D_ATTN))],
      out_specs=pl.BlockSpec((tq, D_ATTN), lambda b, i: (b * nqt + i, 0)),
      out_shape=jax.ShapeDtypeStruct((n, D_ATTN), BF16),
      scratch_shapes=[pltpu.VMEM((2, tq, seq_len), F32)],
      compiler_params=pltpu.CompilerParams(
          dimension_semantics=("arbitrary", "arbitrary"), vmem_limit_bytes=ATTN_VMEM_LIMIT_BYTES),
      name="attn",
  )(q, k, v, g_att)


def _powers(n, xr, xi):
  mag = jnp.exp(n * xr)
  return mag * jnp.cos(n * xi), mag * jnp.sin(n * xi)


def _s5_gen_kernel(lrow_ref, bt_ref, cc_ref, tz_ref, bx_ref, cs_ref, a_ref):
  ch, c, p2 = CHUNK, SSM_GROUP, 2 * SSM_STATE
  is_f = lax.broadcasted_iota(jnp.int32, (ch * c, p2), 1) < SSM_STATE
  n_tab = lax.broadcasted_iota(jnp.int32, (2 * ch, p2), 0).astype(F32)

  def each_row_repeated(tab, idx):
    return jnp.concatenate([jnp.broadcast_to(tab[i:i + 1, :], (c, p2)) for i in idx], axis=0)

  def by_direction(tab, idx_f, idx_b):
    return jnp.where(is_f, each_row_repeated(tab, idx_f), each_row_repeated(tab, idx_b))

  def tiled(tab):
    return jnp.concatenate([tab] * ch, axis=0)

  lre, lim = lrow_ref[0:1, :], lrow_ref[1:2, :]
  dt = jnp.exp(lrow_ref[2:3, :])
  pw_r, pw_i = _powers(n_tab, lre * dt, lim * dt)
  ar, ai = pw_r[1:2, :], pw_i[1:2, :]
  den = lre * lre + lim * lim
  cr = ((ar - 1.0) * lre + ai * lim) / den
  ci = (ai * lre - (ar - 1.0) * lim) / den
  br, bi = tiled(cr * bt_ref[0] - ci * bt_ref[1]), tiled(cr * bt_ref[1] + ci * bt_ref[0])

  up, down = list(range(ch)), [ch - 1 - j for j in range(ch)]
  pr, pi = by_direction(pw_r, down, up), by_direction(pw_i, down, up)
  bx_ref[:, :p2] = (pr * br - pi * bi).astype(BF16)
  bx_ref[:, p2:] = (pr * bi + pi * br).astype(BF16)

  f_idx, b_idx = [i + 1 for i in range(ch)], [ch - i for i in range(ch)]
  pr, pi = by_direction(pw_r, f_idx, b_idx), by_direction(pw_i, f_idx, b_idx)
  tr, ti = tiled(cc_ref[0]), tiled(cc_ref[1])
  cst = jnp.concatenate([tr * pr - ti * pi, -(tr * pi + ti * pr)], axis=1)
  cs_ref[...] = cst.T.astype(BF16)

  pr, pi = each_row_repeated(pw_r, up), each_row_repeated(pw_i, up)
  wr, wi = pr * br - pi * bi, pr * bi + pi * br
  spread = (lax.broadcasted_iota(jnp.int32, (c, ch * c), 1) & (c - 1)
            == lax.broadcasted_iota(jnp.int32, (c, ch * c), 0)).astype(F32)
  nt = (((1,), (1,)), ((), ()))
  split = lambda x: (x.astype(BF16), (x - x.astype(BF16).astype(F32)).astype(BF16))
  spread_b = spread.astype(BF16)
  (cr_h, cr_l), (ci_h, ci_l) = split(cc_ref[0]), split(cc_ref[1])

  def dot3(x, y_h, y_l):
    x_h, x_l = split(x)
    d = lambda a, b: lax.dot_general(a, b, nt, preferred_element_type=F32)
    return d(x_h, y_h) + d(x_h, y_l) + d(x_l, y_h)

  def lag(mask):
    m = dot3(jnp.where(mask, wr, 0.0), cr_h, cr_l) - dot3(jnp.where(mask, wi, 0.0), ci_h, ci_l)
    m_h, m_l = split(m)
    return (jnp.dot(m_h, spread_b, preferred_element_type=F32)
            + jnp.dot(m_l, spread_b, preferred_element_type=F32))

  lag_f, lag_b = lag(is_f), lag(jnp.logical_not(is_f))
  lane_i = lax.broadcasted_iota(jnp.int32, (c, ch * c), 1) >> 4
  for j in range(ch):
    blk = jnp.where(lane_i == j, lag_f[:c, :] + lag_b[:c, :], 0.0)
    for e in range(1, ch - j):
      blk = jnp.where(lane_i == j + e, lag_f[e * c:(e + 1) * c, :], blk)
    for e in range(1, j + 1):
      blk = jnp.where(lane_i == j - e, lag_b[e * c:(e + 1) * c, :], blk)
    tz_ref[j * c:(j + 1) * c, :] = blk.astype(BF16)

  a_ref[0] = jnp.concatenate([pw_r[ch:ch + 1, :]] * 2, axis=1)
  a_ref[1] = jnp.concatenate([pw_i[ch:ch + 1, :]] * 2, axis=1)


def _s5_operators(fwd, bwd):
  g, p, c = N_SSM_GROUPS, SSM_STATE, SSM_GROUP
  both = lambda x, y: jnp.concatenate([x, y], axis=-1)
  ldt = both(jnp.broadcast_to(fwd[2][:, None], (g, p)), jnp.broadcast_to(bwd[2][:, None], (g, p)))
  lrow = jnp.stack([both(fwd[0], bwd[0]), both(fwd[1], bwd[1]), ldt], axis=1)
  bt = jnp.stack([both(jnp.swapaxes(fwd[i], 1, 2), jnp.swapaxes(bwd[i], 1, 2)) for i in (3, 4)], axis=1)
  cc = jnp.stack([both(fwd[i], bwd[i]) for i in (5, 6)], axis=1)
  spec = lambda *s: pl.BlockSpec((None,) + s, lambda i: (i,) + (0,) * len(s))
  tz, bx, cs, a = pl.pallas_call(
      _s5_gen_kernel, grid=(g,),
      in_specs=[spec(3, 2 * p), spec(2, c, 2 * p), spec(2, c, 2 * p)],
      out_specs=[spec(G_LANES, G_LANES)] * 3 + [spec(2, 1, G_LANES)],
      out_shape=[jax.ShapeDtypeStruct((g, G_LANES, G_LANES), BF16)] * 3
      + [jax.ShapeDtypeStruct((g, 2, 1, G_LANES), F32)],
      compiler_params=pltpu.CompilerParams(dimension_semantics=("arbitrary",)),
      name="s5_operators",
  )(lrow, bt, cc)
  trans = jnp.broadcast_to(jnp.swapaxes(a, 0, 1).reshape(2, 1, g * G_LANES), (2, SUBLANES, g * G_LANES))
  return tz, bx, cs, trans


def _step_permutation():
  src = jnp.arange(HALF * LANES)
  j, gl, ci = src // LANES, (src // SSM_GROUP) % GROUPS_PER_TILE, src % SSM_GROUP
  dst = gl * LANES + j * SSM_GROUP + ci
  return (dst[:, None] == jnp.arange(HALF * LANES)[None, :]).astype(BF16)


def _s5_in_kernel(u_ref, perm_ref, bx_ref, up_ref, x_ref, ut_ref):
  nb, tb = SUBLANES, u_ref.shape[1]
  nc = tb // CHUNK
  r = nc * nb
  for b in range(nb):
    ut_ref[pl.ds(b, tb, stride=nb), :] = u_ref[b]
  halves = []
  for h in range(2):
    steps = []
    for j in range(h * HALF, (h + 1) * HALF):
      steps.append(jnp.concatenate(
          [ut_ref[(ci * CHUNK + j) * nb:(ci * CHUNK + j + 1) * nb, :] for ci in range(nc)], axis=0))
    halves.append(jnp.concatenate(steps, axis=1))
  ucat = jnp.concatenate(halves, axis=0).astype(BF16)
  uperm = jnp.dot(ucat, perm_ref[...], preferred_element_type=F32).astype(BF16)
  for gl in range(GROUPS_PER_TILE):
    ug = jnp.concatenate([uperm[h * r:(h + 1) * r, gl * LANES:(gl + 1) * LANES] for h in range(2)], axis=1)
    up_ref[:, gl * G_LANES:(gl + 1) * G_LANES] = ug
    x_ref[:, gl * G_LANES:(gl + 1) * G_LANES] = jnp.dot(ug, bx_ref[gl], preferred_element_type=F32)


def _s5_scan_kernel(x_ref, a_ref, sr_ref, sf_ref, sb_ref):
  nb = SUBLANES
  nr, w = x_ref.shape
  nc = nr // nb
  lane = lax.broadcasted_iota(jnp.int32, (nb, w), 1)
  is_fwd = (lane & (LANES - 1)) < SSM_STATE
  a_re, a_im = a_ref[0], a_ref[1]

  def body(k, s):
    rf = pl.multiple_of(k * nb, nb)
    rb = pl.multiple_of((nc - 1 - k) * nb, nb)
    sf_ref[pl.ds(rf, nb), :] = s
    sb_ref[pl.ds(rb, nb), :] = s
    x = jnp.where(is_fwd, x_ref[pl.ds(rf, nb), :], x_ref[pl.ds(rb, nb), :])
    parts = []
    for g in range(w // G_LANES):
      re = slice(g * G_LANES, g * G_LANES + LANES)
      im = slice(g * G_LANES + LANES, (g + 1) * G_LANES)
      parts.append(a_re[:, re] * s[:, re] - a_im[:, re] * s[:, im] + x[:, re])
      parts.append(a_re[:, im] * s[:, im] + a_im[:, im] * s[:, re] + x[:, im])
    return jnp.concatenate(parts, axis=1)

  lax.fori_loop(0, nc, body, jnp.zeros((nb, w), F32), unroll=4)
  row_is_fwd = (lax.broadcasted_iota(jnp.int32, (1, w), 1) & (LANES - 1)) < SSM_STATE
  sr_ref[...] = jnp.where(row_is_fwd, sf_ref[...], sb_ref[...]).astype(BF16)


def _s5_out_kernel(up_ref, sr_ref, tz_ref, cs_ref, perm_ref, y_ref, yt_ref):
  nb, tb = SUBLANES, y_ref.shape[1]
  nc = tb // CHUNK
  r = nc * nb
  ys = []
  for gl in range(GROUPS_PER_TILE):
    sl = slice(gl * G_LANES, (gl + 1) * G_LANES)
    ys.append(jnp.dot(up_ref[:, sl], tz_ref[gl], preferred_element_type=F32)
              + jnp.dot(sr_ref[:, sl], cs_ref[gl], preferred_element_type=F32))
  ycat = jnp.concatenate(
      [jnp.concatenate([y[:, h * LANES:(h + 1) * LANES] for y in ys], axis=1) for h in range(2)], axis=0)
  hi = ycat.astype(BF16)
  lo = (ycat - hi.astype(F32)).astype(BF16)
  ynat = (jnp.dot(hi, perm_ref[...], preferred_element_type=F32)
          + jnp.dot(lo, perm_ref[...], preferred_element_type=F32))
  for h in range(2):
    for j in range(HALF):
      blk = ynat[h * r:(h + 1) * r, j * LANES:(j + 1) * LANES]
      for ci in range(nc):
        t = ci * CHUNK + h * HALF + j
        yt_ref[t * nb:(t + 1) * nb, :] = blk[ci * nb:(ci + 1) * nb, :]
  for b in range(nb):
    y_ref[b] = yt_ref[pl.ds(b, tb, stride=nb), :]


def _s5_mix(u3, tz, bx, cs, trans, perm):
  nb, seq_len, _ = u3.shape
  assert nb == SUBLANES and seq_len % TB == 0
  nblk = seq_len // TB
  r = TB // CHUNK * nb
  nr = seq_len // CHUNK * nb
  width = N_SSM_GROUPS * G_LANES
  qw = GROUPS_PER_TILE * G_LANES
  seq_spec = pl.BlockSpec((nb, TB, LANES), lambda q, i: (0, i, q))
  row_spec = pl.BlockSpec((r, qw), lambda q, i: (i, q))
  op_spec = pl.BlockSpec((GROUPS_PER_TILE, G_LANES, G_LANES), lambda q, i: (q, 0, 0))
  perm_spec = _const_spec((HALF * LANES, HALF * LANES))
  params = pltpu.CompilerParams(
      dimension_semantics=("arbitrary", "arbitrary"), vmem_limit_bytes=VMEM_LIMIT_BYTES)
  up, x = pl.pallas_call(
      _s5_in_kernel, grid=(N_QUARTERS, nblk),
      in_specs=[seq_spec, perm_spec, op_spec], out_specs=[row_spec, row_spec],
      out_shape=[jax.ShapeDtypeStruct((nr, width), BF16), jax.ShapeDtypeStruct((nr, width), F32)],
      scratch_shapes=[pltpu.VMEM((TB * nb, LANES), F32)],
      compiler_params=params, name="s5_in",
  )(u3, perm, bx)
  sr = pl.pallas_call(
      _s5_scan_kernel, grid=(width // SCAN_W,),
      in_specs=[pl.BlockSpec((nr, SCAN_W), lambda j: (0, j)),
                pl.BlockSpec((2, nb, SCAN_W), lambda j: (0, 0, j))],
      out_specs=pl.BlockSpec((nr, SCAN_W), lambda j: (0, j)),
      out_shape=jax.ShapeDtypeStruct((nr, width), BF16),
      scratch_shapes=[pltpu.VMEM((nr, SCAN_W), F32), pltpu.VMEM((nr, SCAN_W), F32)],
      compiler_params=pltpu.CompilerParams(
          dimension_semantics=("arbitrary",), vmem_limit_bytes=VMEM_LIMIT_BYTES),
      name="s5_scan",
  )(x, trans)
  return pl.pallas_call(
      _s5_out_kernel, grid=(N_QUARTERS, nblk),
      in_specs=[row_spec, row_spec, op_spec, op_spec, perm_spec], out_specs=seq_spec,
      out_shape=jax.ShapeDtypeStruct(u3.shape, F32),
      scratch_shapes=[pltpu.VMEM((TB * nb, LANES), F32)],
      compiler_params=params, name="s5_out",
  )(up, sr, tz, cs, perm.T)


def _post_kernel(x1_ref, u_ref, yc_ref, ya_ref,
                 dskip_ref, wglu_ref, bglu_ref, gssm_ref, wos_ref, woa_ref, gpost_ref,
                 g2pre_ref, wgu_ref, wd_ref, g2post_ref, o_ref):
  rows_per = x1_ref.shape[0] // ROW_SPLIT
  for part in range(ROW_SPLIT):
    rows = slice(part * rows_per, (part + 1) * rows_per)
    y = yc_ref[rows, :] + dskip_ref[...] * u_ref[rows, :]
    y = y * (0.5 * (1.0 + jnp.tanh(math.sqrt(2.0 / math.pi) * (y + 0.044715 * (y * y * y)))))
    gate = jnp.dot(y.astype(BF16), wglu_ref[...], preferred_element_type=F32) + bglu_ref[...]
    y = y * jax.nn.sigmoid(gate)
    y_ssm = _rms(y, gssm_ref[...]).astype(BF16)
    m = (jnp.dot(y_ssm, wos_ref[...], preferred_element_type=F32)
         + jnp.dot(ya_ref[rows, :], woa_ref[...], preferred_element_type=F32))
    x2 = x1_ref[rows, :] + _rms(m, gpost_ref[...])
    h = _rms(x2, g2pre_ref[...]).astype(BF16)
    f = _swiglu(h, wgu_ref, wd_ref)
    o_ref[rows, :] = x2 + 0.5 * _rms(f, g2post_ref[...])


def _post_call(x1, u, yc, ya, w):
  n = x1.shape[0]
  row = lambda i: (i, 0)
  in_specs = [
      pl.BlockSpec((TM, D_MODEL), row), pl.BlockSpec((TM, D_SSM), row),
      pl.BlockSpec((TM, D_SSM), row), pl.BlockSpec((TM, D_ATTN), row),
      _const_spec((1, D_SSM)), _const_spec((D_SSM, D_SSM)), _const_spec((1, D_SSM)),
      _const_spec((1, D_SSM)), _const_spec((D_SSM, D_MODEL)), _const_spec((D_ATTN, D_MODEL)),
      _const_spec((1, D_MODEL)), _const_spec((1, D_MODEL)),
      _const_spec((D_MODEL, 2 * D_FF)), _const_spec((D_FF, D_MODEL)), _const_spec((1, D_MODEL)),
  ]
  return pl.pallas_call(
      _post_kernel, grid=(n // TM,), in_specs=in_specs,
      out_specs=pl.BlockSpec((TM, D_MODEL), row),
      out_shape=jax.ShapeDtypeStruct((n, D_MODEL), F32),
      compiler_params=pltpu.CompilerParams(
          dimension_semantics=("arbitrary",), vmem_limit_bytes=VMEM_LIMIT_BYTES),
      name="post",
  )(x1, u, yc, ya,
    w["d_skip"], w["w_glu"], w["b_glu"], w["g_ssm_out"], w["w_out_ssm"], w["w_out_att"],
    w["g_mix_post"], w["g_ffn2_pre"], w["wgu2"], w["wd2"], w["g_ffn2_post"])


def _rope_tables(length):
  f32 = np.float32
  inv = f32(1.0) / (f32(ROPE_THETA) ** (np.arange(0, QK_ROPE, 2, dtype=f32) / f32(QK_ROPE)))
  ang = np.arange(length, dtype=f32)[:, None] * inv[None, :]
  cos, sin = np.cos(ang), np.sin(ang)
  zk = np.zeros((length, LANES - QK_ROPE), f32)
  ck = np.concatenate([cos, cos, zk], axis=1)
  sk = np.concatenate([-sin, sin, zk], axis=1)
  qs = f32((QK_NOPE + QK_ROPE) ** -0.5 * LOG2E)
  zq = np.zeros((length, LANES - QK_NOPE - QK_ROPE), f32)
  cq = qs * np.concatenate([np.ones((length, QK_NOPE), f32), cos, cos, zq], axis=1)
  sq = qs * np.concatenate([np.zeros((length, QK_NOPE), f32), -sin, sin, zq], axis=1)
  return {"cq": cq, "sq": sq, "ck": ck, "sk": sk}


def _head_tiles(wmat, width):
  k = wmat.shape[0]
  wh = wmat.reshape(k, N_HEADS, width)
  return jnp.pad(wh, ((0, 0), (0, 0), (0, HEAD_PAD - width))).reshape(k, D_HEADS_PAD)


def kernel(x_prompt, x_sample, g_ffn1_pre, w_ffn1_gate, w_ffn1_up, w_ffn1_down, g_ffn1_post, g_mix_pre, w_in, lam_re_fwd, lam_im_fwd, log_dt_fwd, b_re_fwd, b_im_fwd, c_re_fwd, c_im_fwd, lam_re_bwd, lam_im_bwd, log_dt_bwd, b_re_bwd, b_im_bwd, c_re_bwd, c_im_bwd, d_skip, w_glu, b_glu, g_ssm_out, g_q, w_uq, g_kv, w_ukv, g_att_out, w_out, g_mix_post, g_ffn2_pre, w_ffn2_gate, w_ffn2_up, w_ffn2_down, g_ffn2_post):
  depth = w_in.shape[0]
  tables = {}
  perm = _step_permutation()
  s5_ops = [
      _s5_operators(
          (lam_re_fwd[l], lam_im_fwd[l], log_dt_fwd[l], b_re_fwd[l], b_im_fwd[l], c_re_fwd[l], c_im_fwd[l]),
          (lam_re_bwd[l], lam_im_bwd[l], log_dt_bwd[l], b_re_bwd[l], b_im_bwd[l], c_re_bwd[l], c_im_bwd[l]))
      for l in range(depth)]

  def trunk(x):
    batch, seq_len, _ = x.shape
    assert batch == SUBLANES and seq_len % TM == 0 and seq_len % TB == 0
    if seq_len not in tables:
      tables[seq_len] = _rope_tables(seq_len)
    h = x.reshape(batch * seq_len, D_MODEL)
    for l in range(depth):
      row = lambda v: v[l].reshape(1, -1)
      win = w_in[l]
      win_pad = jnp.concatenate(
          [win, jnp.zeros((D_MODEL, D_IN_PAD - win.shape[1]), F32)], axis=1).astype(BF16)
      wukv = w_ukv[l].reshape(KV_RANK, N_HEADS, QK_NOPE + V_HEAD)
      w = {
          "g_ffn1_pre": row(g_ffn1_pre),
          "wgu1": jnp.concatenate([w_ffn1_gate[l], w_ffn1_up[l]], axis=1).astype(BF16),
          "wd1": w_ffn1_down[l].astype(BF16), "g_ffn1_post": row(g_ffn1_post),
          "g_mix_pre": row(g_mix_pre), "w_in": win_pad,
          "g_q": row(g_q), "w_uq": _head_tiles(w_uq[l], QK_NOPE + QK_ROPE).astype(BF16),
          "g_kv": row(g_kv),
          "w_ukv": jnp.concatenate(
              [_head_tiles(wukv[:, :, :QK_NOPE].reshape(KV_RANK, -1), QK_NOPE),
               _head_tiles(wukv[:, :, QK_NOPE:].reshape(KV_RANK, -1), V_HEAD)], axis=1).astype(BF16),
          "d_skip": row(d_skip), "w_glu": w_glu[l].astype(BF16), "b_glu": row(b_glu),
          "g_ssm_out": row(g_ssm_out),
          "w_out_ssm": w_out[l][:D_SSM].astype(BF16), "w_out_att": w_out[l][D_SSM:].astype(BF16),
          "g_mix_post": row(g_mix_post), "g_ffn2_pre": row(g_ffn2_pre),
          "wgu2": jnp.concatenate([w_ffn2_gate[l], w_ffn2_up[l]], axis=1).astype(BF16),
          "wd2": w_ffn2_down[l].astype(BF16), "g_ffn2_post": row(g_ffn2_post),
      }
      x1, u, q, k, v = _pre_call(h, seq_len, tables[seq_len], w)
      ya = _attn_call(q, k, v, row(g_att_out), batch, seq_len)
      u3 = u.reshape(batch, seq_len, D_SSM)
      yc = _s5_mix(u3, *s5_ops[l], perm).reshape(u.shape)
      h = _post_call(x1, u, yc, ya, w)
    return h.reshape(batch, seq_len, D_MODEL)

  return (trunk(x_prompt), trunk(x_sample))
```

```python
import functools
import math

import jax
import jax.numpy as jnp
import numpy as np
from jax import lax
from jax.experimental import pallas as pl
from jax.experimental.pallas import tpu as pltpu

F32 = jnp.float32
BF16 = jnp.bfloat16

D_MODEL = 1024
D_FF = 2816
D_SSM = 512
SSM_GROUP = 16
N_SSM_GROUPS = 32
SSM_STATE = 64
N_HEADS = 8
QK_NOPE = 64
QK_ROPE = 32
V_HEAD = 64
D_ATTN = N_HEADS * V_HEAD
Q_RANK = 384
KV_RANK = 256
ROPE_THETA = 10000.0
EPS = 1e-6
LOG2E = math.log2(math.e)

LANES = 128
SUBLANES = 8
VMEM_LIMIT_BYTES = 56 * 1024 * 1024

HEAD_PAD = LANES
D_HEADS_PAD = N_HEADS * HEAD_PAD
D_IN_PAD = D_SSM + Q_RANK + KV_RANK + LANES
MXU_DIM = 256
FF_CHUNKS = ((0, 6 * MXU_DIM), (6 * MXU_DIM, D_FF))
ROW_SPLIT = 2

TM = 512
TM_POST = 1024
SCORE_BYTES = 16 * 1024 * 1024
WIDE_VMEM_LIMIT_BYTES = 60 * 1024 * 1024
GROUPS_PER_TILE = LANES // SSM_GROUP
N_QUARTERS = D_SSM // LANES
CHUNK = 16
HALF = CHUNK // 2
G_LANES = CHUNK * SSM_GROUP
TB = 512
SCAN_W = 512


def _const_spec(shape):
  zeros = (0,) * len(shape)
  return pl.BlockSpec(shape, lambda *_: zeros, pipeline_mode=pl.Buffered(1))


def _rms(x, g):
  ms = jnp.mean(x * x, axis=-1, keepdims=True)
  return x * lax.rsqrt(ms + EPS) * g


def _swiglu(h, wg_ref, wu_ref, wd_ref):
  acc = None
  for lo, hi in FF_CHUNKS:
    gate = jnp.dot(h, wg_ref[:, lo:hi], preferred_element_type=F32)
    up = jnp.dot(h, wu_ref[:, lo:hi], preferred_element_type=F32)
    act = (gate * jax.nn.sigmoid(gate) * up).astype(BF16)
    part = jnp.dot(act, wd_ref[lo:hi, :], preferred_element_type=F32)
    acc = part if acc is None else acc + part
  return acc


def _pre_kernel(x_ref, cq_ref, sq_ref, ck_ref, sk_ref,
                g1pre_ref, wg_ref, wu_ref, wd_ref, g1post_ref, gmix_ref, win_ref,
                gq_ref, wuq_ref, gkv_ref, wukv_ref,
                x1_ref, u_ref, q_ref, k_ref, v_ref, x1_prev_ref):
  @pl.when(pl.program_id(0) == 0)
  def _():
    x1_prev_ref[...] = jnp.zeros_like(x1_prev_ref)

  rows_per = x_ref.shape[0] // ROW_SPLIT
  half = QK_ROPE // 2
  lane = lax.broadcasted_iota(jnp.int32, (rows_per, LANES), 1)
  ones_col = (lane == V_HEAD).astype(F32)
  for part in range(ROW_SPLIT):
    rows = slice(part * rows_per, (part + 1) * rows_per)
    h2 = _rms(x1_prev_ref[rows, :], gmix_ref[...]).astype(BF16)
    z = jnp.dot(h2, win_ref[...], preferred_element_type=F32)
    u_ref[rows, :] = z[:, :D_SSM]
    q_c = z[:, D_SSM:D_SSM + Q_RANK]
    kv_c = z[:, D_SSM + Q_RANK:D_SSM + Q_RANK + KV_RANK]
    kr = z[:, D_SSM + Q_RANK + KV_RANK:]

    kr_sw = jnp.where(lane < half, pltpu.roll(kr, LANES - half, 1), pltpu.roll(kr, half, 1))
    k_pe = pltpu.roll(kr * ck_ref[rows, :] + kr_sw * sk_ref[rows, :], QK_NOPE, 1)

    qn = _rms(q_c, gq_ref[...]).astype(BF16)
    q_raw = jnp.dot(qn, wuq_ref[...], preferred_element_type=F32)
    kvn = _rms(kv_c, gkv_ref[...]).astype(BF16)
    kv_raw = jnp.dot(kvn, wukv_ref[...], preferred_element_type=F32)

    cq = cq_ref[rows, :]
    sq = sq_ref[rows, :]
    for hd in range(N_HEADS):
      lo = hd * HEAD_PAD
      t = q_raw[:, lo:lo + HEAD_PAD]
      t_sw = jnp.where(lane < QK_NOPE + half, pltpu.roll(t, LANES - half, 1), pltpu.roll(t, half, 1))
      q_ref[rows, lo:lo + HEAD_PAD] = (t * cq + t_sw * sq).astype(BF16)
      kv = kv_raw[:, lo:lo + HEAD_PAD]
      k_ref[rows, lo:lo + HEAD_PAD] = jnp.where(lane < QK_NOPE, kv, k_pe).astype(BF16)
      v_ref[rows, lo:lo + HEAD_PAD] = jnp.where(
          lane < V_HEAD, pltpu.roll(kv, LANES - QK_NOPE, 1), ones_col).astype(BF16)

    x = x_ref[rows, :]
    h = _rms(x, g1pre_ref[...]).astype(BF16)
    f = _swiglu(h, wg_ref, wu_ref, wd_ref)
    x1 = x + 0.5 * _rms(f, g1post_ref[...])
    x1_ref[rows, :] = x1
    x1_prev_ref[rows, :] = x1


def _pre_call(x2d, seq_len, tabs, w):
  n = x2d.shape[0]
  nt = n // TM
  nlt = seq_len // TM
  cur = lambda i: (jnp.minimum(i, nt - 1), 0)
  prev = lambda i: (jnp.maximum(i - 1, 0), 0)
  pos = lambda i: (jnp.maximum(i - 1, 0) % nlt, 0)
  tab_spec = pl.BlockSpec((TM, LANES), pos)
  in_specs = [
      pl.BlockSpec((TM, D_MODEL), cur), tab_spec, tab_spec, tab_spec, tab_spec,
      _const_spec((1, D_MODEL)), _const_spec((D_MODEL, D_FF)), _const_spec((D_MODEL, D_FF)),
      _const_spec((D_FF, D_MODEL)),
      _const_spec((1, D_MODEL)), _const_spec((1, D_MODEL)), _const_spec((D_MODEL, D_IN_PAD)),
      _const_spec((1, Q_RANK)), _const_spec((Q_RANK, D_HEADS_PAD)),
      _const_spec((1, KV_RANK)), _const_spec((KV_RANK, D_HEADS_PAD)),
  ]
  out_shape = [
      jax.ShapeDtypeStruct((n, D_MODEL), F32), jax.ShapeDtypeStruct((n, D_SSM), F32),
      jax.ShapeDtypeStruct((n, D_HEADS_PAD), BF16), jax.ShapeDtypeStruct((n, D_HEADS_PAD), BF16),
      jax.ShapeDtypeStruct((n, D_HEADS_PAD), BF16),
  ]
  out_specs = [
      pl.BlockSpec((TM, D_MODEL), cur), pl.BlockSpec((TM, D_SSM), prev),
      pl.BlockSpec((TM, D_HEADS_PAD), prev), pl.BlockSpec((TM, D_HEADS_PAD), prev),
      pl.BlockSpec((TM, D_HEADS_PAD), prev),
  ]
  return pl.pallas_call(
      _pre_kernel, grid=(nt + 1,), in_specs=in_specs, out_specs=out_specs, out_shape=out_shape,
      scratch_shapes=[pltpu.VMEM((TM, D_MODEL), F32)],
      compiler_params=pltpu.CompilerParams(
          dimension_semantics=("arbitrary",), vmem_limit_bytes=VMEM_LIMIT_BYTES),
      name="pre",
  )(x2d, tabs["cq"], tabs["sq"], tabs["ck"], tabs["sk"],
    w["g_ffn1_pre"], w["wg1"], w["wu1"], w["wd1"], w["g_ffn1_post"], w["g_mix_pre"], w["w_in"],
    w["g_q"], w["w_uq"], w["g_kv"], w["w_ukv"])


def _attn_kernel(q_ref, k_ref, v_ref, g_ref, o_ref, s_ref):
  tq = q_ref.shape[0]
  lane = lax.broadcasted_iota(jnp.int32, (tq, HEAD_PAD), 1)

  def scores(hd):
    lo = hd * HEAD_PAD
    s = lax.dot_general(q_ref[:, lo:lo + HEAD_PAD], k_ref[:, lo:lo + HEAD_PAD],
                        (((1,), (1,)), ((), ())), preferred_element_type=F32)
    s_ref[hd % 2] = s
    return jnp.max(s, axis=-1, keepdims=True)

  heads = []
  ssq = jnp.zeros((tq, 1), F32)
  m_next = scores(0)
  for hd in range(N_HEADS):
    lo = hd * HEAD_PAD
    m = m_next
    if hd + 1 < N_HEADS:
      m_next = scores(hd + 1)
    p = jnp.exp2(s_ref[hd % 2] - m).astype(BF16)
    oa = jnp.dot(p, v_ref[:, lo:lo + HEAD_PAD], preferred_element_type=F32)
    o = jnp.where(lane < V_HEAD, oa / oa[:, V_HEAD:V_HEAD + 1], 0.0)
    ssq = ssq + jnp.sum(o * o, axis=-1, keepdims=True)
    heads.append(o)
  inv = lax.rsqrt(ssq * (1.0 / D_ATTN) + EPS)
  for j in range(N_HEADS // 2):
    pair = heads[2 * j] + pltpu.roll(heads[2 * j + 1], V_HEAD, 1)
    o_ref[:, j * LANES:(j + 1) * LANES] = (
        pair * inv * g_ref[:, j * LANES:(j + 1) * LANES]).astype(BF16)


def _attn_call(q, k, v, g_att, batch, seq_len):
  n = q.shape[0]
  tq = min(seq_len, SCORE_BYTES // (2 * seq_len * 4))
  assert seq_len % tq == 0
  nqt = seq_len // tq
  kv_spec = pl.BlockSpec((seq_len, D_HEADS_PAD), lambda b, i: (b, 0))
  return pl.pallas_call(
      _attn_kernel, grid=(batch, nqt),
      in_specs=[pl.BlockSpec((tq, D_HEADS_PAD), lambda b, i: (b * nqt + i, 0)), kv_spec, kv_spec,
                _const_spec((1, D_ATTN))],
      out_specs=pl.BlockSpec((tq, D_ATTN), lambda b, i: (b * nqt + i, 0)),
      out_shape=jax.ShapeDtypeStruct((n, D_ATTN), BF16),
      scratch_shapes=[pltpu.VMEM((2, tq, seq_len), F32)],
      compiler_params=pltpu.CompilerParams(
          dimension_semantics=("arbitrary", "arbitrary"), vmem_limit_bytes=WIDE_VMEM_LIMIT_BYTES),
      name="attn",
  )(q, k, v, g_att)


def _powers(n, xr, xi):
  mag = jnp.exp(n * xr)
  return mag * jnp.cos(n * xi), mag * jnp.sin(n * xi)


def _s5_gen_kernel(lrow_ref, bt_ref, cc_ref, tz_ref, bx_ref, cs_ref, a_ref):
  ch, c, p2 = CHUNK, SSM_GROUP, 2 * SSM_STATE
  is_f = lax.broadcasted_iota(jnp.int32, (ch * c, p2), 1) < SSM_STATE
  n_tab = lax.broadcasted_iota(jnp.int32, (2 * ch, p2), 0).astype(F32)

  def each_row_repeated(tab, idx):
    return jnp.concatenate([jnp.broadcast_to(tab[i:i + 1, :], (c, p2)) for i in idx], axis=0)

  def by_direction(tab, idx_f, idx_b):
    return jnp.where(is_f, each_row_repeated(tab, idx_f), each_row_repeated(tab, idx_b))

  def tiled(tab):
    return jnp.concatenate([tab] * ch, axis=0)

  lre, lim = lrow_ref[0:1, :], lrow_ref[1:2, :]
  dt = jnp.exp(lrow_ref[2:3, :])
  pw_r, pw_i = _powers(n_tab, lre * dt, lim * dt)
  ar, ai = pw_r[1:2, :], pw_i[1:2, :]
  den = lre * lre + lim * lim
  cr = ((ar - 1.0) * lre + ai * lim) / den
  ci = (ai * lre - (ar - 1.0) * lim) / den
  br, bi = tiled(cr * bt_ref[0] - ci * bt_ref[1]), tiled(cr * bt_ref[1] + ci * bt_ref[0])

  up, down = list(range(ch)), [ch - 1 - j for j in range(ch)]
  pr, pi = by_direction(pw_r, down, up), by_direction(pw_i, down, up)
  bx_ref[:, :p2] = (pr * br - pi * bi).astype(BF16)
  bx_ref[:, p2:] = (pr * bi + pi * br).astype(BF16)

  f_idx, b_idx = [i + 1 for i in range(ch)], [ch - i for i in range(ch)]
  pr, pi = by_direction(pw_r, f_idx, b_idx), by_direction(pw_i, f_idx, b_idx)
  tr, ti = tiled(cc_ref[0]), tiled(cc_ref[1])
  cst = jnp.concatenate([tr * pr - ti * pi, -(tr * pi + ti * pr)], axis=1)
  cs_ref[...] = cst.T.astype(BF16)

  pr, pi = each_row_repeated(pw_r, up), each_row_repeated(pw_i, up)
  wr, wi = pr * br - pi * bi, pr * bi + pi * br
  spread = (lax.broadcasted_iota(jnp.int32, (c, ch * c), 1) & (c - 1)
            == lax.broadcasted_iota(jnp.int32, (c, ch * c), 0)).astype(F32)
  nt = (((1,), (1,)), ((), ()))
  split = lambda x: (x.astype(BF16), (x - x.astype(BF16).astype(F32)).astype(BF16))
  spread_b = spread.astype(BF16)
  (cr_h, cr_l), (ci_h, ci_l) = split(cc_ref[0]), split(cc_ref[1])

  def dot3(x, y_h, y_l):
    x_h, x_l = split(x)
    d = lambda a, b: lax.dot_general(a, b, nt, preferred_element_type=F32)
    return d(x_h, y_h) + d(x_h, y_l) + d(x_l, y_h)

  def lag(mask):
    m = dot3(jnp.where(mask, wr, 0.0), cr_h, cr_l) - dot3(jnp.where(mask, wi, 0.0), ci_h, ci_l)
    m_h, m_l = split(m)
    return (jnp.dot(m_h, spread_b, preferred_element_type=F32)
            + jnp.dot(m_l, spread_b, preferred_element_type=F32))

  lag_f, lag_b = lag(is_f), lag(jnp.logical_not(is_f))
  lane_i = lax.broadcasted_iota(jnp.int32, (c, ch * c), 1) >> 4
  for j in range(ch):
    blk = jnp.where(lane_i == j, lag_f[:c, :] + lag_b[:c, :], 0.0)
    for e in range(1, ch - j):
      blk = jnp.where(lane_i == j + e, lag_f[e * c:(e + 1) * c, :], blk)
    for e in range(1, j + 1):
      blk = jnp.where(lane_i == j - e, lag_b[e * c:(e + 1) * c, :], blk)
    tz_ref[j * c:(j + 1) * c, :] = blk.astype(BF16)

  a_ref[0] = jnp.concatenate([pw_r[ch:ch + 1, :]] * 2, axis=1)
  a_ref[1] = jnp.concatenate([pw_i[ch:ch + 1, :]] * 2, axis=1)


def _s5_operators(fwd, bwd):
  g, p, c = N_SSM_GROUPS, SSM_STATE, SSM_GROUP
  both = lambda x, y: jnp.concatenate([x, y], axis=-1)
  ldt = both(jnp.broadcast_to(fwd[2][:, None], (g, p)), jnp.broadcast_to(bwd[2][:, None], (g, p)))
  lrow = jnp.stack([both(fwd[0], bwd[0]), both(fwd[1], bwd[1]), ldt], axis=1)
  bt = jnp.stack([both(jnp.swapaxes(fwd[i], 1, 2), jnp.swapaxes(bwd[i], 1, 2)) for i in (3, 4)], axis=1)
  cc = jnp.stack([both(fwd[i], bwd[i]) for i in (5, 6)], axis=1)
  spec = lambda *s: pl.BlockSpec((None,) + s, lambda i: (i,) + (0,) * len(s))
  tz, bx, cs, a = pl.pallas_call(
      _s5_gen_kernel, grid=(g,),
      in_specs=[spec(3, 2 * p), spec(2, c, 2 * p), spec(2, c, 2 * p)],
      out_specs=[spec(G_LANES, G_LANES)] * 3 + [spec(2, 1, G_LANES)],
      out_shape=[jax.ShapeDtypeStruct((g, G_LANES, G_LANES), BF16)] * 3
      + [jax.ShapeDtypeStruct((g, 2, 1, G_LANES), F32)],
      compiler_params=pltpu.CompilerParams(dimension_semantics=("arbitrary",)),
      name="s5_operators",
  )(lrow, bt, cc)
  trans = jnp.broadcast_to(jnp.swapaxes(a, 0, 1).reshape(2, 1, g * G_LANES), (2, SUBLANES, g * G_LANES))
  return tz, bx, cs, trans


def _step_permutation():
  src = jnp.arange(HALF * LANES)
  j, gl, ci = src // LANES, (src // SSM_GROUP) % GROUPS_PER_TILE, src % SSM_GROUP
  dst = gl * LANES + j * SSM_GROUP + ci
  return (dst[:, None] == jnp.arange(HALF * LANES)[None, :]).astype(BF16)


def _s5_in_kernel(u_ref, perm_ref, bx_ref, up_ref, x_ref, ut_ref):
  nb, tb = SUBLANES, u_ref.shape[1]
  nc = tb // CHUNK
  r = nc * nb
  for b in range(nb):
    ut_ref[pl.ds(b, tb, stride=nb), :] = u_ref[b]
  halves = []
  for h in range(2):
    steps = []
    for j in range(h * HALF, (h + 1) * HALF):
      steps.append(jnp.concatenate(
          [ut_ref[(ci * CHUNK + j) * nb:(ci * CHUNK + j + 1) * nb, :] for ci in range(nc)], axis=0))
    halves.append(jnp.concatenate(steps, axis=1))
  ucat = jnp.concatenate(halves, axis=0).astype(BF16)
  uperm = jnp.dot(ucat, perm_ref[...], preferred_element_type=F32).astype(BF16)
  for gl in range(GROUPS_PER_TILE):
    ug = jnp.concatenate([uperm[h * r:(h + 1) * r, gl * LANES:(gl + 1) * LANES] for h in range(2)], axis=1)
    up_ref[:, gl * G_LANES:(gl + 1) * G_LANES] = ug
    x_ref[:, gl * G_LANES:(gl + 1) * G_LANES] = jnp.dot(ug, bx_ref[gl], preferred_element_type=F32)


def _s5_scan_kernel(x_ref, a_ref, sr_ref, sf_ref, sb_ref):
  nb = SUBLANES
  nr, w = x_ref.shape
  nc = nr // nb
  lane = lax.broadcasted_iota(jnp.int32, (nb, w), 1)
  is_fwd = (lane & (LANES - 1)) < SSM_STATE
  a_re, a_im = a_ref[0], a_ref[1]

  def body(k, s):
    rf = pl.multiple_of(k * nb, nb)
    rb = pl.multiple_of((nc - 1 - k) * nb, nb)
    sf_ref[pl.ds(rf, nb), :] = s
    sb_ref[pl.ds(rb, nb), :] = s
    x = jnp.where(is_fwd, x_ref[pl.ds(rf, nb), :], x_ref[pl.ds(rb, nb), :])
    parts = []
    for g in range(w // G_LANES):
      re = slice(g * G_LANES, g * G_LANES + LANES)
      im = slice(g * G_LANES + LANES, (g + 1) * G_LANES)
      parts.append(a_re[:, re] * s[:, re] - a_im[:, re] * s[:, im] + x[:, re])
      parts.append(a_re[:, im] * s[:, im] + a_im[:, im] * s[:, re] + x[:, im])
    return jnp.concatenate(parts, axis=1)

  lax.fori_loop(0, nc, body, jnp.zeros((nb, w), F32), unroll=4)
  row_is_fwd = (lax.broadcasted_iota(jnp.int32, (1, w), 1) & (LANES - 1)) < SSM_STATE
  sr_ref[...] = jnp.where(row_is_fwd, sf_ref[...], sb_ref[...]).astype(BF16)


def _s5_out_kernel(up_ref, sr_ref, tz_ref, cs_ref, perm_ref, y_ref, yt_ref):
  nb, tb = SUBLANES, y_ref.shape[1]
  nc = tb // CHUNK
  r = nc * nb
  ys = []
  for gl in range(GROUPS_PER_TILE):
    sl = slice(gl * G_LANES, (gl + 1) * G_LANES)
    ys.append(jnp.dot(up_ref[:, sl], tz_ref[gl], preferred_element_type=F32)
              + jnp.dot(sr_ref[:, sl], cs_ref[gl], preferred_element_type=F32))
  ycat = jnp.concatenate(
      [jnp.concatenate([y[:, h * LANES:(h + 1) * LANES] for y in ys], axis=1) for h in range(2)], axis=0)
  hi = ycat.astype(BF16)
  lo = (ycat - hi.astype(F32)).astype(BF16)
  ynat = (jnp.dot(hi, perm_ref[...], preferred_element_type=F32)
          + jnp.dot(lo, perm_ref[...], preferred_element_type=F32))
  for h in range(2):
    for j in range(HALF):
      blk = ynat[h * r:(h + 1) * r, j * LANES:(j + 1) * LANES]
      for ci in range(nc):
        t = ci * CHUNK + h * HALF + j
        yt_ref[t * nb:(t + 1) * nb, :] = blk[ci * nb:(ci + 1) * nb, :]
  for b in range(nb):
    y_ref[b] = yt_ref[pl.ds(b, tb, stride=nb), :]


def _s5_mix(u3, tz, bx, cs, trans, perm):
  nb, seq_len, _ = u3.shape
  assert nb == SUBLANES and seq_len % TB == 0
  nblk = seq_len // TB
  r = TB // CHUNK * nb
  nr = seq_len // CHUNK * nb
  width = N_SSM_GROUPS * G_LANES
  qw = GROUPS_PER_TILE * G_LANES
  seq_spec = pl.BlockSpec((nb, TB, LANES), lambda q, i: (0, i, q))
  row_spec = pl.BlockSpec((r, qw), lambda q, i: (i, q))
  op_spec = pl.BlockSpec((GROUPS_PER_TILE, G_LANES, G_LANES), lambda q, i: (q, 0, 0))
  perm_spec = _const_spec((HALF * LANES, HALF * LANES))
  params = pltpu.CompilerParams(
      dimension_semantics=("arbitrary", "arbitrary"), vmem_limit_bytes=VMEM_LIMIT_BYTES)
  up, x = pl.pallas_call(
      _s5_in_kernel, grid=(N_QUARTERS, nblk),
      in_specs=[seq_spec, perm_spec, op_spec], out_specs=[row_spec, row_spec],
      out_shape=[jax.ShapeDtypeStruct((nr, width), BF16), jax.ShapeDtypeStruct((nr, width), F32)],
      scratch_shapes=[pltpu.VMEM((TB * nb, LANES), F32)],
      compiler_params=params, name="s5_in",
  )(u3, perm, bx)
  sr = pl.pallas_call(
      _s5_scan_kernel, grid=(width // SCAN_W,),
      in_specs=[pl.BlockSpec((nr, SCAN_W), lambda j: (0, j)),
                pl.BlockSpec((2, nb, SCAN_W), lambda j: (0, 0, j))],
      out_specs=pl.BlockSpec((nr, SCAN_W), lambda j: (0, j)),
      out_shape=jax.ShapeDtypeStruct((nr, width), BF16),
      scratch_shapes=[pltpu.VMEM((nr, SCAN_W), F32), pltpu.VMEM((nr, SCAN_W), F32)],
      compiler_params=pltpu.CompilerParams(
          dimension_semantics=("arbitrary",), vmem_limit_bytes=VMEM_LIMIT_BYTES),
      name="s5_scan",
  )(x, trans)
  return pl.pallas_call(
      _s5_out_kernel, grid=(N_QUARTERS, nblk),
      in_specs=[row_spec, row_spec, op_spec, op_spec, perm_spec], out_specs=seq_spec,
      out_shape=jax.ShapeDtypeStruct(u3.shape, F32),
      scratch_shapes=[pltpu.VMEM((TB * nb, LANES), F32)],
      compiler_params=params, name="s5_out",
  )(up, sr, tz, cs, perm.T)


def _post_kernel(x1_ref, u_ref, yc_ref, ya_ref,
                 dskip_ref, wglu_ref, bglu_ref, gssm_ref, wos_ref, woa_ref, gpost_ref,
                 g2pre_ref, wg_ref, wu_ref, wd_ref, g2post_ref, o_ref):
  rows_per = x1_ref.shape[0] // ROW_SPLIT
  for part in range(ROW_SPLIT):
    rows = slice(part * rows_per, (part + 1) * rows_per)
    y = yc_ref[rows, :] + dskip_ref[...] * u_ref[rows, :]
    y = y * (0.5 * (1.0 + jnp.tanh(math.sqrt(2.0 / math.pi) * (y + 0.044715 * (y * y * y)))))
    gate = jnp.dot(y.astype(BF16), wglu_ref[...], preferred_element_type=F32) + bglu_ref[...]
    y = y * jax.nn.sigmoid(gate)
    y_ssm = _rms(y, gssm_ref[...]).astype(BF16)
    m = (jnp.dot(y_ssm, wos_ref[...], preferred_element_type=F32)
         + jnp.dot(ya_ref[rows, :], woa_ref[...], preferred_element_type=F32))
    x2 = x1_ref[rows, :] + _rms(m, gpost_ref[...])
    h = _rms(x2, g2pre_ref[...]).astype(BF16)
    f = _swiglu(h, wg_ref, wu_ref, wd_ref)
    o_ref[rows, :] = x2 + 0.5 * _rms(f, g2post_ref[...])


def _post_call(x1, u, yc, ya, w):
  n = x1.shape[0]
  row = lambda i: (i, 0)
  tm = TM_POST
  in_specs = [
      pl.BlockSpec((tm, D_MODEL), row), pl.BlockSpec((tm, D_SSM), row),
      pl.BlockSpec((tm, D_SSM), row), pl.BlockSpec((tm, D_ATTN), row),
      _const_spec((1, D_SSM)), _const_spec((D_SSM, D_SSM)), _const_spec((1, D_SSM)),
      _const_spec((1, D_SSM)), _const_spec((D_SSM, D_MODEL)), _const_spec((D_ATTN, D_MODEL)),
      _const_spec((1, D_MODEL)), _const_spec((1, D_MODEL)),
      _const_spec((D_MODEL, D_FF)), _const_spec((D_MODEL, D_FF)), _const_spec((D_FF, D_MODEL)),
      _const_spec((1, D_MODEL)),
  ]
  return pl.pallas_call(
      _post_kernel, grid=(n // tm,), in_specs=in_specs,
      out_specs=pl.BlockSpec((tm, D_MODEL), row),
      out_shape=jax.ShapeDtypeStruct((n, D_MODEL), F32),
      compiler_params=pltpu.CompilerParams(
          dimension_semantics=("arbitrary",), vmem_limit_bytes=WIDE_VMEM_LIMIT_BYTES),
      name="post",
  )(x1, u, yc, ya,
    w["d_skip"], w["w_glu"], w["b_glu"], w["g_ssm_out"], w["w_out_ssm"], w["w_out_att"],
    w["g_mix_post"], w["g_ffn2_pre"], w["wg2"], w["wu2"], w["wd2"], w["g_ffn2_post"])


def _rope_tables(length):
  f32 = np.float32
  inv = f32(1.0) / (f32(ROPE_THETA) ** (np.arange(0, QK_ROPE, 2, dtype=f32) / f32(QK_ROPE)))
  ang = np.arange(length, dtype=f32)[:, None] * inv[None, :]
  cos, sin = np.cos(ang), np.sin(ang)
  zk = np.zeros((length, LANES - QK_ROPE), f32)
  ck = np.concatenate([cos, cos, zk], axis=1)
  sk = np.concatenate([-sin, sin, zk], axis=1)
  qs = f32((QK_NOPE + QK_ROPE) ** -0.5 * LOG2E)
  zq = np.zeros((length, LANES - QK_NOPE - QK_ROPE), f32)
  cq = qs * np.concatenate([np.ones((length, QK_NOPE), f32), cos, cos, zq], axis=1)
  sq = qs * np.concatenate([np.zeros((length, QK_NOPE), f32), -sin, sin, zq], axis=1)
  return {"cq": cq, "sq": sq, "ck": ck, "sk": sk}


def _head_tiles(wmat, width):
  k = wmat.shape[0]
  wh = wmat.reshape(k, N_HEADS, width)
  return jnp.pad(wh, ((0, 0), (0, 0), (0, HEAD_PAD - width))).reshape(k, D_HEADS_PAD)


def kernel(x_prompt, x_sample, g_ffn1_pre, w_ffn1_gate, w_ffn1_up, w_ffn1_down, g_ffn1_post, g_mix_pre, w_in, lam_re_fwd, lam_im_fwd, log_dt_fwd, b_re_fwd, b_im_fwd, c_re_fwd, c_im_fwd, lam_re_bwd, lam_im_bwd, log_dt_bwd, b_re_bwd, b_im_bwd, c_re_bwd, c_im_bwd, d_skip, w_glu, b_glu, g_ssm_out, g_q, w_uq, g_kv, w_ukv, g_att_out, w_out, g_mix_post, g_ffn2_pre, w_ffn2_gate, w_ffn2_up, w_ffn2_down, g_ffn2_post):
  depth = w_in.shape[0]
  tables = {}
  perm = _step_permutation()
  s5_ops = [
      _s5_operators(
          (lam_re_fwd[l], lam_im_fwd[l], log_dt_fwd[l], b_re_fwd[l], b_im_fwd[l], c_re_fwd[l], c_im_fwd[l]),
          (lam_re_bwd[l], lam_im_bwd[l], log_dt_bwd[l], b_re_bwd[l], b_im_bwd[l], c_re_bwd[l], c_im_bwd[l]))
      for l in range(depth)]

  def trunk(x):
    batch, seq_len, _ = x.shape
    assert batch == SUBLANES and seq_len % TM == 0 and seq_len % TB == 0
    if seq_len not in tables:
      tables[seq_len] = _rope_tables(seq_len)
    h = x.reshape(batch * seq_len, D_MODEL)
    for l in range(depth):
      row = lambda v: v[l].reshape(1, -1)
      win = w_in[l]
      win_pad = jnp.concatenate(
          [win, jnp.zeros((D_MODEL, D_IN_PAD - win.shape[1]), F32)], axis=1).astype(BF16)
      w = {
          "g_ffn1_pre": row(g_ffn1_pre),
          "wg1": w_ffn1_gate[l].astype(BF16), "wu1": w_ffn1_up[l].astype(BF16),
          "wd1": w_ffn1_down[l].astype(BF16), "g_ffn1_post": row(g_ffn1_post),
          "g_mix_pre": row(g_mix_pre), "w_in": win_pad,
          "g_q": row(g_q), "w_uq": _head_tiles(w_uq[l], QK_NOPE + QK_ROPE).astype(BF16),
          "g_kv": row(g_kv), "w_ukv": w_ukv[l].astype(BF16),
          "d_skip": row(d_skip), "w_glu": w_glu[l].astype(BF16), "b_glu": row(b_glu),
          "g_ssm_out": row(g_ssm_out),
          "w_out_ssm": w_out[l][:D_SSM].astype(BF16), "w_out_att": w_out[l][D_SSM:].astype(BF16),
          "g_mix_post": row(g_mix_post), "g_ffn2_pre": row(g_ffn2_pre),
          "wg2": w_ffn2_gate[l].astype(BF16), "wu2": w_ffn2_up[l].astype(BF16),
          "wd2": w_ffn2_down[l].astype(BF16), "g_ffn2_post": row(g_ffn2_post),
      }
      x1, u, q, k, v = _pre_call(h, seq_len, tables[seq_len], w)
      ya = _attn_call(q, k, v, row(g_att_out), batch, seq_len)
      u3 = u.reshape(batch, seq_len, D_SSM)
      yc = _s5_mix(u3, *s5_ops[l], perm).reshape(u.shape)
      h = _post_call(x1, u, yc, ya, w)
    return h.reshape(batch, seq_len, D_MODEL)

  return (trunk(x_prompt), trunk(x_sample))
```

```python
import functools
import math

import jax
import jax.numpy as jnp
import numpy as np
from jax import lax
from jax.experimental import pallas as pl
from jax.experimental.pallas import tpu as pltpu

F32 = jnp.float32
BF16 = jnp.bfloat16

D_MODEL = 1024
D_FF = 2816
D_SSM = 512
SSM_GROUP = 16
N_SSM_GROUPS = 32
SSM_STATE = 64
N_HEADS = 8
QK_NOPE = 64
QK_ROPE = 32
V_HEAD = 64
D_ATTN = N_HEADS * V_HEAD
Q_RANK = 384
KV_RANK = 256
ROPE_THETA = 10000.0
EPS = 1e-6
LOG2E = math.log2(math.e)

LANES = 128
SUBLANES = 8
VMEM_LIMIT_BYTES = 56 * 1024 * 1024

HEAD_PAD = LANES
D_HEADS_PAD = N_HEADS * HEAD_PAD
D_IN_PAD = D_SSM + Q_RANK + KV_RANK + LANES
MXU_DIM = 256
FF_CHUNKS = ((0, 6 * MXU_DIM), (6 * MXU_DIM, D_FF))
ROW_SPLIT = 2
PRE_ROW_SPLIT = 1

TM = 512
TM_POST = 1024
SCORE_BYTES = 16 * 1024 * 1024
WIDE_VMEM_LIMIT_BYTES = 60 * 1024 * 1024
GROUPS_PER_TILE = LANES // SSM_GROUP
N_QUARTERS = D_SSM // LANES
CHUNK = 16
HALF = CHUNK // 2
G_LANES = CHUNK * SSM_GROUP
TB = 512
SCAN_W = 512


def _const_spec(shape):
  zeros = (0,) * len(shape)
  return pl.BlockSpec(shape, lambda *_: zeros, pipeline_mode=pl.Buffered(1))


def _rms(x, g):
  ms = jnp.mean(x * x, axis=-1, keepdims=True)
  return x * lax.rsqrt(ms + EPS) * g


def _swiglu(h, wg_ref, wu_ref, wd_ref):
  acc = None
  for lo, hi in FF_CHUNKS:
    gate = jnp.dot(h, wg_ref[:, lo:hi], preferred_element_type=F32)
    up = jnp.dot(h, wu_ref[:, lo:hi], preferred_element_type=F32)
    act = (gate * jax.nn.sigmoid(gate) * up).astype(BF16)
    part = jnp.dot(act, wd_ref[lo:hi, :], preferred_element_type=F32)
    acc = part if acc is None else acc + part
  return acc


def _pre_kernel(x_ref, cq_ref, sq_ref, ck_ref, sk_ref,
                g1pre_ref, wg_ref, wu_ref, wd_ref, g1post_ref, gmix_ref, win_ref,
                gq_ref, wuq_ref, gkv_ref, wukv_ref,
                x1_ref, u_ref, q_ref, k_ref, v_ref, x1_prev_ref):
  @pl.when(pl.program_id(0) == 0)
  def _():
    x1_prev_ref[...] = jnp.zeros_like(x1_prev_ref)

  rows_per = x_ref.shape[0] // PRE_ROW_SPLIT
  half = QK_ROPE // 2
  lane = lax.broadcasted_iota(jnp.int32, (rows_per, LANES), 1)
  ones_col = (lane == V_HEAD).astype(F32)
  for part in range(PRE_ROW_SPLIT):
    rows = slice(part * rows_per, (part + 1) * rows_per)
    h2 = _rms(x1_prev_ref[rows, :], gmix_ref[...]).astype(BF16)
    z = jnp.dot(h2, win_ref[...], preferred_element_type=F32)
    u_ref[rows, :] = z[:, :D_SSM]
    q_c = z[:, D_SSM:D_SSM + Q_RANK]
    kv_c = z[:, D_SSM + Q_RANK:D_SSM + Q_RANK + KV_RANK]
    kr = z[:, D_SSM + Q_RANK + KV_RANK:]

    kr_sw = jnp.where(lane < half, pltpu.roll(kr, LANES - half, 1), pltpu.roll(kr, half, 1))
    k_pe = pltpu.roll(kr * ck_ref[rows, :] + kr_sw * sk_ref[rows, :], QK_NOPE, 1)

    qn = _rms(q_c, gq_ref[...]).astype(BF16)
    q_raw = jnp.dot(qn, wuq_ref[...], preferred_element_type=F32)
    kvn = _rms(kv_c, gkv_ref[...]).astype(BF16)
    kv_raw = jnp.dot(kvn, wukv_ref[...], preferred_element_type=F32)

    cq = cq_ref[rows, :]
    sq = sq_ref[rows, :]
    for hd in range(N_HEADS):
      lo = hd * HEAD_PAD
      t = q_raw[:, lo:lo + HEAD_PAD]
      t_sw = jnp.where(lane < QK_NOPE + half, pltpu.roll(t, LANES - half, 1), pltpu.roll(t, half, 1))
      q_ref[rows, lo:lo + HEAD_PAD] = (t * cq + t_sw * sq).astype(BF16)
      kv = kv_raw[:, lo:lo + HEAD_PAD]
      k_ref[rows, lo:lo + HEAD_PAD] = jnp.where(lane < QK_NOPE, kv, k_pe).astype(BF16)
      v_ref[rows, lo:lo + HEAD_PAD] = jnp.where(
          lane < V_HEAD, pltpu.roll(kv, LANES - QK_NOPE, 1), ones_col).astype(BF16)

    x = x_ref[rows, :]
    h = _rms(x, g1pre_ref[...]).astype(BF16)
    f = _swiglu(h, wg_ref, wu_ref, wd_ref)
    x1 = x + 0.5 * _rms(f, g1post_ref[...])
    x1_ref[rows, :] = x1
    x1_prev_ref[rows, :] = x1


def _pre_call(x2d, seq_len, tabs, w):
  n = x2d.shape[0]
  nt = n // TM
  nlt = seq_len // TM
  cur = lambda i: (jnp.minimum(i, nt - 1), 0)
  prev = lambda i: (jnp.maximum(i - 1, 0), 0)
  pos = lambda i: (jnp.maximum(i - 1, 0) % nlt, 0)
  tab_spec = pl.BlockSpec((TM, LANES), pos)
  in_specs = [
      pl.BlockSpec((TM, D_MODEL), cur), tab_spec, tab_spec, tab_spec, tab_spec,
      _const_spec((1, D_MODEL)), _const_spec((D_MODEL, D_FF)), _const_spec((D_MODEL, D_FF)),
      _const_spec((D_FF, D_MODEL)),
      _const_spec((1, D_MODEL)), _const_spec((1, D_MODEL)), _const_spec((D_MODEL, D_IN_PAD)),
      _const_spec((1, Q_RANK)), _const_spec((Q_RANK, D_HEADS_PAD)),
      _const_spec((1, KV_RANK)), _const_spec((KV_RANK, D_HEADS_PAD)),
  ]
  out_shape = [
      jax.ShapeDtypeStruct((n, D_MODEL), F32), jax.ShapeDtypeStruct((n, D_SSM), F32),
      jax.ShapeDtypeStruct((n, D_HEADS_PAD), BF16), jax.ShapeDtypeStruct((n, D_HEADS_PAD), BF16),
      jax.ShapeDtypeStruct((n, D_HEADS_PAD), BF16),
  ]
  out_specs = [
      pl.BlockSpec((TM, D_MODEL), cur), pl.BlockSpec((TM, D_SSM), prev),
      pl.BlockSpec((TM, D_HEADS_PAD), prev), pl.BlockSpec((TM, D_HEADS_PAD), prev),
      pl.BlockSpec((TM, D_HEADS_PAD), prev),
  ]
  return pl.pallas_call(
      _pre_kernel, grid=(nt + 1,), in_specs=in_specs, out_specs=out_specs, out_shape=out_shape,
      scratch_shapes=[pltpu.VMEM((TM, D_MODEL), F32)],
      compiler_params=pltpu.CompilerParams(
          dimension_semantics=("arbitrary",), vmem_limit_bytes=VMEM_LIMIT_BYTES),
      name="pre",
  )(x2d, tabs["cq"], tabs["sq"], tabs["ck"], tabs["sk"],
    w["g_ffn1_pre"], w["wg1"], w["wu1"], w["wd1"], w["g_ffn1_post"], w["g_mix_pre"], w["w_in"],
    w["g_q"], w["w_uq"], w["g_kv"], w["w_ukv"])


def _attn_kernel(q_ref, k_ref, v_ref, g_ref, o_ref, s_ref):
  tq = q_ref.shape[0]
  lane = lax.broadcasted_iota(jnp.int32, (tq, HEAD_PAD), 1)

  def scores(hd):
    lo = hd * HEAD_PAD
    s = lax.dot_general(q_ref[:, lo:lo + HEAD_PAD], k_ref[:, lo:lo + HEAD_PAD],
                        (((1,), (1,)), ((), ())), preferred_element_type=F32)
    s_ref[hd % 2] = s
    return jnp.max(s, axis=-1, keepdims=True)

  heads = []
  ssq = jnp.zeros((tq, 1), F32)
  m_next = scores(0)
  for hd in range(N_HEADS):
    lo = hd * HEAD_PAD
    m = m_next
    if hd + 1 < N_HEADS:
      m_next = scores(hd + 1)
    p = jnp.exp2(s_ref[hd % 2] - m).astype(BF16)
    oa = jnp.dot(p, v_ref[:, lo:lo + HEAD_PAD], preferred_element_type=F32)
    o = jnp.where(lane < V_HEAD, oa / oa[:, V_HEAD:V_HEAD + 1], 0.0)
    ssq = ssq + jnp.sum(o * o, axis=-1, keepdims=True)
    heads.append(o)
  inv = lax.rsqrt(ssq * (1.0 / D_ATTN) + EPS)
  for j in range(N_HEADS // 2):
    pair = heads[2 * j] + pltpu.roll(heads[2 * j + 1], V_HEAD, 1)
    o_ref[:, j * LANES:(j + 1) * LANES] = (
        pair * inv * g_ref[:, j * LANES:(j + 1) * LANES]).astype(BF16)


def _attn_call(q, k, v, g_att, batch, seq_len):
  n = q.shape[0]
  tq = min(seq_len, SCORE_BYTES // (2 * seq_len * 4))
  assert seq_len % tq == 0
  nqt = seq_len // tq
  kv_spec = pl.BlockSpec((seq_len, D_HEADS_PAD), lambda b, i: (b, 0))
  return pl.pallas_call(
      _attn_kernel, grid=(batch, nqt),
      in_specs=[pl.BlockSpec((tq, D_HEADS_PAD), lambda b, i: (b * nqt + i, 0)), kv_spec, kv_spec,
                _const_spec((1, D_ATTN))],
      out_specs=pl.BlockSpec((tq, D_ATTN), lambda b, i: (b * nqt + i, 0)),
      out_shape=jax.ShapeDtypeStruct((n, D_ATTN), BF16),
      scratch_shapes=[pltpu.VMEM((2, tq, seq_len), F32)],
      compiler_params=pltpu.CompilerParams(
          dimension_semantics=("arbitrary", "arbitrary"), vmem_limit_bytes=WIDE_VMEM_LIMIT_BYTES),
      name="attn",
  )(q, k, v, g_att)


def _powers(n, xr, xi):
  mag = jnp.exp(n * xr)
  return mag * jnp.cos(n * xi), mag * jnp.sin(n * xi)


def _s5_gen_kernel(lrow_ref, bt_ref, cc_ref, tz_ref, bx_ref, cs_ref, a_ref):
  ch, c, p2 = CHUNK, SSM_GROUP, 2 * SSM_STATE
  is_f = lax.broadcasted_iota(jnp.int32, (ch * c, p2), 1) < SSM_STATE
  n_tab = lax.broadcasted_iota(jnp.int32, (2 * ch, p2), 0).astype(F32)

  def each_row_repeated(tab, idx):
    return jnp.concatenate([jnp.broadcast_to(tab[i:i + 1, :], (c, p2)) for i in idx], axis=0)

  def by_direction(tab, idx_f, idx_b):
    return jnp.where(is_f, each_row_repeated(tab, idx_f), each_row_repeated(tab, idx_b))

  def tiled(tab):
    return jnp.concatenate([tab] * ch, axis=0)

  lre, lim = lrow_ref[0:1, :], lrow_ref[1:2, :]
  dt = jnp.exp(lrow_ref[2:3, :])
  pw_r, pw_i = _powers(n_tab, lre * dt, lim * dt)
  ar, ai = pw_r[1:2, :], pw_i[1:2, :]
  den = lre * lre + lim * lim
  cr = ((ar - 1.0) * lre + ai * lim) / den
  ci = (ai * lre - (ar - 1.0) * lim) / den
  br, bi = tiled(cr * bt_ref[0] - ci * bt_ref[1]), tiled(cr * bt_ref[1] + ci * bt_ref[0])

  up, down = list(range(ch)), [ch - 1 - j for j in range(ch)]
  pr, pi = by_direction(pw_r, down, up), by_direction(pw_i, down, up)
  bx_ref[:, :p2] = (pr * br - pi * bi).astype(BF16)
  bx_ref[:, p2:] = (pr * bi + pi * br).astype(BF16)

  f_idx, b_idx = [i + 1 for i in range(ch)], [ch - i for i in range(ch)]
  pr, pi = by_direction(pw_r, f_idx, b_idx), by_direction(pw_i, f_idx, b_idx)
  tr, ti = tiled(cc_ref[0]), tiled(cc_ref[1])
  cst = jnp.concatenate([tr * pr - ti * pi, -(tr * pi + ti * pr)], axis=1)
  cs_ref[...] = cst.T.astype(BF16)

  pr, pi = each_row_repeated(pw_r, up), each_row_repeated(pw_i, up)
  wr, wi = pr * br - pi * bi, pr * bi + pi * br
  spread = (lax.broadcasted_iota(jnp.int32, (c, ch * c), 1) & (c - 1)
            == lax.broadcasted_iota(jnp.int32, (c, ch * c), 0)).astype(F32)
  nt = (((1,), (1,)), ((), ()))
  split = lambda x: (x.astype(BF16), (x - x.astype(BF16).astype(F32)).astype(BF16))
  spread_b = spread.astype(BF16)
  (cr_h, cr_l), (ci_h, ci_l) = split(cc_ref[0]), split(cc_ref[1])

  def dot3(x, y_h, y_l):
    x_h, x_l = split(x)
    d = lambda a, b: lax.dot_general(a, b, nt, preferred_element_type=F32)
    return d(x_h, y_h) + d(x_h, y_l) + d(x_l, y_h)

  def lag(mask):
    m = dot3(jnp.where(mask, wr, 0.0), cr_h, cr_l) - dot3(jnp.where(mask, wi, 0.0), ci_h, ci_l)
    m_h, m_l = split(m)
    return (jnp.dot(m_h, spread_b, preferred_element_type=F32)
            + jnp.dot(m_l, spread_b, preferred_element_type=F32))

  lag_f, lag_b = lag(is_f), lag(jnp.logical_not(is_f))
  lane_i = lax.broadcasted_iota(jnp.int32, (c, ch * c), 1) >> 4
  for j in range(ch):
    blk = jnp.where(lane_i == j, lag_f[:c, :] + lag_b[:c, :], 0.0)
    for e in range(1, ch - j):
      blk = jnp.where(lane_i == j + e, lag_f[e * c:(e + 1) * c, :], blk)
    for e in range(1, j + 1):
      blk = jnp.where(lane_i == j - e, lag_b[e * c:(e + 1) * c, :], blk)
    tz_ref[j * c:(j + 1) * c, :] = blk.astype(BF16)

  a_ref[0] = jnp.concatenate([pw_r[ch:ch + 1, :]] * 2, axis=1)
  a_ref[1] = jnp.concatenate([pw_i[ch:ch + 1, :]] * 2, axis=1)


def _s5_operators(fwd, bwd):
  g, p, c = N_SSM_GROUPS, SSM_STATE, SSM_GROUP
  both = lambda x, y: jnp.concatenate([x, y], axis=-1)
  ldt = both(jnp.broadcast_to(fwd[2][:, None], (g, p)), jnp.broadcast_to(bwd[2][:, None], (g, p)))
  lrow = jnp.stack([both(fwd[0], bwd[0]), both(fwd[1], bwd[1]), ldt], axis=1)
  bt = jnp.stack([both(jnp.swapaxes(fwd[i], 1, 2), jnp.swapaxes(bwd[i], 1, 2)) for i in (3, 4)], axis=1)
  cc = jnp.stack([both(fwd[i], bwd[i]) for i in (5, 6)], axis=1)
  spec = lambda *s: pl.BlockSpec((None,) + s, lambda i: (i,) + (0,) * len(s))
  tz, bx, cs, a = pl.pallas_call(
      _s5_gen_kernel, grid=(g,),
      in_specs=[spec(3, 2 * p), spec(2, c, 2 * p), spec(2, c, 2 * p)],
      out_specs=[spec(G_LANES, G_LANES)] * 3 + [spec(2, 1, G_LANES)],
      out_shape=[jax.ShapeDtypeStruct((g, G_LANES, G_LANES), BF16)] * 3
      + [jax.ShapeDtypeStruct((g, 2, 1, G_LANES), F32)],
      compiler_params=pltpu.CompilerParams(dimension_semantics=("arbitrary",)),
      name="s5_operators",
  )(lrow, bt, cc)
  trans = jnp.broadcast_to(jnp.swapaxes(a, 0, 1).reshape(2, 1, g * G_LANES), (2, SUBLANES, g * G_LANES))
  return tz, bx, cs, trans


def _step_permutation():
  src = jnp.arange(HALF * LANES)
  j, gl, ci = src // LANES, (src // SSM_GROUP) % GROUPS_PER_TILE, src % SSM_GROUP
  dst = gl * LANES + j * SSM_GROUP + ci
  return (dst[:, None] == jnp.arange(HALF * LANES)[None, :]).astype(BF16)


def _s5_in_kernel(u_ref, perm_ref, bx_ref, up_ref, x_ref, ut_ref):
  nb, tb = SUBLANES, u_ref.shape[1]
  nc = tb // CHUNK
  r = nc * nb
  for b in range(nb):
    ut_ref[pl.ds(b, tb, stride=nb), :] = u_ref[b]
  halves = []
  for h in range(2):
    steps = []
    for j in range(h * HALF, (h + 1) * HALF):
      steps.append(jnp.concatenate(
          [ut_ref[(ci * CHUNK + j) * nb:(ci * CHUNK + j + 1) * nb, :] for ci in range(nc)], axis=0))
    halves.append(jnp.concatenate(steps, axis=1))
  ucat = jnp.concatenate(halves, axis=0).astype(BF16)
  uperm = jnp.dot(ucat, perm_ref[...], preferred_element_type=F32).astype(BF16)
  for gl in range(GROUPS_PER_TILE):
    ug = jnp.concatenate([uperm[h * r:(h + 1) * r, gl * LANES:(gl + 1) * LANES] for h in range(2)], axis=1)
    up_ref[:, gl * G_LANES:(gl + 1) * G_LANES] = ug
    x_ref[:, gl * G_LANES:(gl + 1) * G_LANES] = jnp.dot(ug, bx_ref[gl], preferred_element_type=F32)


def _s5_scan_kernel(x_ref, a_ref, sr_ref, sf_ref, sb_ref):
  nb = SUBLANES
  nr, w = x_ref.shape
  nc = nr // nb
  lane = lax.broadcasted_iota(jnp.int32, (nb, w), 1)
  is_fwd = (lane & (LANES - 1)) < SSM_STATE
  a_re, a_im = a_ref[0], a_ref[1]

  def body(k, s):
    rf = pl.multiple_of(k * nb, nb)
    rb = pl.multiple_of((nc - 1 - k) * nb, nb)
    sf_ref[pl.ds(rf, nb), :] = s
    sb_ref[pl.ds(rb, nb), :] = s
    x = jnp.where(is_fwd, x_ref[pl.ds(rf, nb), :], x_ref[pl.ds(rb, nb), :])
    parts = []
    for g in range(w // G_LANES):
      re = slice(g * G_LANES, g * G_LANES + LANES)
      im = slice(g * G_LANES + LANES, (g + 1) * G_LANES)
      parts.append(a_re[:, re] * s[:, re] - a_im[:, re] * s[:, im] + x[:, re])
      parts.append(a_re[:, im] * s[:, im] + a_im[:, im] * s[:, re] + x[:, im])
    return jnp.concatenate(parts, axis=1)

  lax.fori_loop(0, nc, body, jnp.zeros((nb, w), F32), unroll=4)
  row_is_fwd = (lax.broadcasted_iota(jnp.int32, (1, w), 1) & (LANES - 1)) < SSM_STATE
  sr_ref[...] = jnp.where(row_is_fwd, sf_ref[...], sb_ref[...]).astype(BF16)


def _s5_out_kernel(up_ref, sr_ref, tz_ref, cs_ref, perm_ref, y_ref, yt_ref):
  nb, tb = SUBLANES, y_ref.shape[1]
  nc = tb // CHUNK
  r = nc * nb
  ys = []
  for gl in range(GROUPS_PER_TILE):
    sl = slice(gl * G_LANES, (gl + 1) * G_LANES)
    ys.append(jnp.dot(up_ref[:, sl], tz_ref[gl], preferred_element_type=F32)
              + jnp.dot(sr_ref[:, sl], cs_ref[gl], preferred_element_type=F32))
  ycat = jnp.concatenate(
      [jnp.concatenate([y[:, h * LANES:(h + 1) * LANES] for y in ys], axis=1) for h in range(2)], axis=0)
  hi = ycat.astype(BF16)
  lo = (ycat - hi.astype(F32)).astype(BF16)
  ynat = (jnp.dot(hi, perm_ref[...], preferred_element_type=F32)
          + jnp.dot(lo, perm_ref[...], preferred_element_type=F32))
  for h in range(2):
    for j in range(HALF):
      blk = ynat[h * r:(h + 1) * r, j * LANES:(j + 1) * LANES]
      for ci in range(nc):
        t = ci * CHUNK + h * HALF + j
        yt_ref[t * nb:(t + 1) * nb, :] = blk[ci * nb:(ci + 1) * nb, :]
  for b in range(nb):
    y_ref[b] = yt_ref[pl.ds(b, tb, stride=nb), :]


def _s5_mix(u3, tz, bx, cs, trans, perm):
  nb, seq_len, _ = u3.shape
  assert nb == SUBLANES and seq_len % TB == 0
  nblk = seq_len // TB
  r = TB // CHUNK * nb
  nr = seq_len // CHUNK * nb
  width = N_SSM_GROUPS * G_LANES
  qw = GROUPS_PER_TILE * G_LANES
  seq_spec = pl.BlockSpec((nb, TB, LANES), lambda q, i: (0, i, q))
  row_spec = pl.BlockSpec((r, qw), lambda q, i: (i, q))
  op_spec = pl.BlockSpec((GROUPS_PER_TILE, G_LANES, G_LANES), lambda q, i: (q, 0, 0))
  perm_spec = _const_spec((HALF * LANES, HALF * LANES))
  params = pltpu.CompilerParams(
      dimension_semantics=("arbitrary", "arbitrary"), vmem_limit_bytes=VMEM_LIMIT_BYTES)
  up, x = pl.pallas_call(
      _s5_in_kernel, grid=(N_QUARTERS, nblk),
      in_specs=[seq_spec, perm_spec, op_spec], out_specs=[row_spec, row_spec],
      out_shape=[jax.ShapeDtypeStruct((nr, width), BF16), jax.ShapeDtypeStruct((nr, width), F32)],
      scratch_shapes=[pltpu.VMEM((TB * nb, LANES), F32)],
      compiler_params=params, name="s5_in",
  )(u3, perm, bx)
  sr = pl.pallas_call(
      _s5_scan_kernel, grid=(width // SCAN_W,),
      in_specs=[pl.BlockSpec((nr, SCAN_W), lambda j: (0, j)),
                pl.BlockSpec((2, nb, SCAN_W), lambda j: (0, 0, j))],
      out_specs=pl.BlockSpec((nr, SCAN_W), lambda j: (0, j)),
      out_shape=jax.ShapeDtypeStruct((nr, width), BF16),
      scratch_shapes=[pltpu.VMEM((nr, SCAN_W), F32), pltpu.VMEM((nr, SCAN_W), F32)],
      compiler_params=pltpu.CompilerParams(
          dimension_semantics=("arbitrary",), vmem_limit_bytes=VMEM_LIMIT_BYTES),
      name="s5_scan",
  )(x, trans)
  return pl.pallas_call(
      _s5_out_kernel, grid=(N_QUARTERS, nblk),
      in_specs=[row_spec, row_spec, op_spec, op_spec, perm_spec], out_specs=seq_spec,
      out_shape=jax.ShapeDtypeStruct(u3.shape, F32),
      scratch_shapes=[pltpu.VMEM((TB * nb, LANES), F32)],
      compiler_params=params, name="s5_out",
  )(up, sr, tz, cs, perm.T)


def _post_kernel(x1_ref, u_ref, yc_ref, ya_ref,
                 dskip_ref, wglu_ref, bglu_ref, gssm_ref, wos_ref, woa_ref, gpost_ref,
                 g2pre_ref, wg_ref, wu_ref, wd_ref, g2post_ref, o_ref):
  rows_per = x1_ref.shape[0] // ROW_SPLIT
  for part in range(ROW_SPLIT):
    rows = slice(part * rows_per, (part + 1) * rows_per)
    y = yc_ref[rows, :] + dskip_ref[...] * u_ref[rows, :]
    y = y * (0.5 * (1.0 + jnp.tanh(math.sqrt(2.0 / math.pi) * (y + 0.044715 * (y * y * y)))))
    gate = jnp.dot(y.astype(BF16), wglu_ref[...], preferred_element_type=F32) + bglu_ref[...]
    y = y * jax.nn.sigmoid(gate)
    y_ssm = _rms(y, gssm_ref[...]).astype(BF16)
    m = (jnp.dot(y_ssm, wos_ref[...], preferred_element_type=F32)
         + jnp.dot(ya_ref[rows, :], woa_ref[...], preferred_element_type=F32))
    x2 = x1_ref[rows, :] + _rms(m, gpost_ref[...])
    h = _rms(x2, g2pre_ref[...]).astype(BF16)
    f = _swiglu(h, wg_ref, wu_ref, wd_ref)
    o_ref[rows, :] = x2 + 0.5 * _rms(f, g2post_ref[...])


def _post_call(x1, u, yc, ya, w):
  n = x1.shape[0]
  row = lambda i: (i, 0)
  tm = TM_POST
  in_specs = [
      pl.BlockSpec((tm, D_MODEL), row), pl.BlockSpec((tm, D_SSM), row),
      pl.BlockSpec((tm, D_SSM), row), pl.BlockSpec((tm, D_ATTN), row),
      _const_spec((1, D_SSM)), _const_spec((D_SSM, D_SSM)), _const_spec((1, D_SSM)),
      _const_spec((1, D_SSM)), _const_spec((D_SSM, D_MODEL)), _const_spec((D_ATTN, D_MODEL)),
      _const_spec((1, D_MODEL)), _const_spec((1, D_MODEL)),
      _const_spec((D_MODEL, D_FF)), _const_spec((D_MODEL, D_FF)), _const_spec((D_FF, D_MODEL)),
      _const_spec((1, D_MODEL)),
  ]
  return pl.pallas_call(
      _post_kernel, grid=(n // tm,), in_specs=in_specs,
      out_specs=pl.BlockSpec((tm, D_MODEL), row),
      out_shape=jax.ShapeDtypeStruct((n, D_MODEL), F32),
      compiler_params=pltpu.CompilerParams(
          dimension_semantics=("arbitrary",), vmem_limit_bytes=WIDE_VMEM_LIMIT_BYTES),
      name="post",
  )(x1, u, yc, ya,
    w["d_skip"], w["w_glu"], w["b_glu"], w["g_ssm_out"], w["w_out_ssm"], w["w_out_att"],
    w["g_mix_post"], w["g_ffn2_pre"], w["wg2"], w["wu2"], w["wd2"], w["g_ffn2_post"])


def _rope_tables(length):
  f32 = np.float32
  inv = f32(1.0) / (f32(ROPE_THETA) ** (np.arange(0, QK_ROPE, 2, dtype=f32) / f32(QK_ROPE)))
  ang = np.arange(length, dtype=f32)[:, None] * inv[None, :]
  cos, sin = np.cos(ang), np.sin(ang)
  zk = np.zeros((length, LANES - QK_ROPE), f32)
  ck = np.concatenate([cos, cos, zk], axis=1)
  sk = np.concatenate([-sin, sin, zk], axis=1)
  qs = f32((QK_NOPE + QK_ROPE) ** -0.5 * LOG2E)
  zq = np.zeros((length, LANES - QK_NOPE - QK_ROPE), f32)
  cq = qs * np.concatenate([np.ones((length, QK_NOPE), f32), cos, cos, zq], axis=1)
  sq = qs * np.concatenate([np.zeros((length, QK_NOPE), f32), -sin, sin, zq], axis=1)
  return {"cq": cq, "sq": sq, "ck": ck, "sk": sk}


def _head_tiles(wmat, width):
  k = wmat.shape[0]
  wh = wmat.reshape(k, N_HEADS, width)
  return jnp.pad(wh, ((0, 0), (0, 0), (0, HEAD_PAD - width))).reshape(k, D_HEADS_PAD)


def kernel(x_prompt, x_sample, g_ffn1_pre, w_ffn1_gate, w_ffn1_up, w_ffn1_down, g_ffn1_post, g_mix_pre, w_in, lam_re_fwd, lam_im_fwd, log_dt_fwd, b_re_fwd, b_im_fwd, c_re_fwd, c_im_fwd, lam_re_bwd, lam_im_bwd, log_dt_bwd, b_re_bwd, b_im_bwd, c_re_bwd, c_im_bwd, d_skip, w_glu, b_glu, g_ssm_out, g_q, w_uq, g_kv, w_ukv, g_att_out, w_out, g_mix_post, g_ffn2_pre, w_ffn2_gate, w_ffn2_up, w_ffn2_down, g_ffn2_post):
  depth = w_in.shape[0]
  tables = {}
  perm = _step_permutation()
  s5_ops = [
      _s5_operators(
          (lam_re_fwd[l], lam_im_fwd[l], log_dt_fwd[l], b_re_fwd[l], b_im_fwd[l], c_re_fwd[l], c_im_fwd[l]),
          (lam_re_bwd[l], lam_im_bwd[l], log_dt_bwd[l], b_re_bwd[l], b_im_bwd[l], c_re_bwd[l], c_im_bwd[l]))
      for l in range(depth)]

  def trunk(x):
    batch, seq_len, _ = x.shape
    assert batch == SUBLANES and seq_len % TM == 0 and seq_len % TB == 0
    if seq_len not in tables:
      tables[seq_len] = _rope_tables(seq_len)
    h = x.reshape(batch * seq_len, D_MODEL)
    for l in range(depth):
      row = lambda v: v[l].reshape(1, -1)
      win = w_in[l]
      win_pad = jnp.concatenate(
          [win, jnp.zeros((D_MODEL, D_IN_PAD - win.shape[1]), F32)], axis=1).astype(BF16)
      w = {
          "g_ffn1_pre": row(g_ffn1_pre),
          "wg1": w_ffn1_gate[l].astype(BF16), "wu1": w_ffn1_up[l].astype(BF16),
          "wd1": w_ffn1_down[l].astype(BF16), "g_ffn1_post": row(g_ffn1_post),
          "g_mix_pre": row(g_mix_pre), "w_in": win_pad,
          "g_q": row(g_q), "w_uq": _head_tiles(w_uq[l], QK_NOPE + QK_ROPE).astype(BF16),
          "g_kv": row(g_kv), "w_ukv": w_ukv[l].astype(BF16),
          "d_skip": row(d_skip), "w_glu": w_glu[l].astype(BF16), "b_glu": row(b_glu),
          "g_ssm_out": row(g_ssm_out),
          "w_out_ssm": w_out[l][:D_SSM].astype(BF16), "w_out_att": w_out[l][D_SSM:].astype(BF16),
          "g_mix_post": row(g_mix_post), "g_ffn2_pre": row(g_ffn2_pre),
          "wg2": w_ffn2_gate[l].astype(BF16), "wu2": w_ffn2_up[l].astype(BF16),
          "wd2": w_ffn2_down[l].astype(BF16), "g_ffn2_post": row(g_ffn2_post),
      }
      x1, u, q, k, v = _pre_call(h, seq_len, tables[seq_len], w)
      ya = _attn_call(q, k, v, row(g_att_out), batch, seq_len)
      u3 = u.reshape(batch, seq_len, D_SSM)
      yc = _s5_mix(u3, *s5_ops[l], perm).reshape(u.shape)
      h = _post_call(x1, u, yc, ya, w)
    return h.reshape(batch, seq_len, D_MODEL)

  return (trunk(x_prompt), trunk(x_sample))
```
